```python
import jax, jax.numpy as jnp
from jax import lax
import numpy as np

D_MODEL = 1024
BATCH = 8
SEQ = 4096
DEPTH = 2

N_EVEN = (DEPTH + 1) // 2
N_ODD = DEPTH // 2
D_A = D_MODEL // 2
CONV_WIDTH = 31
CONV_PAD = CONV_WIDTH // 2
N_POOL = 4
POOL_WINDOWS = (2, 4, 8, 16)
D_B = D_MODEL // 2
G_POOL = D_B // N_POOL
N_FOURIER = 4
D_C = D_MODEL // 2
H_C = D_C // N_FOURIER
N_SGU = 4
D_D = D_MODEL // 2
H_D = D_D // N_SGU
CHUNK = 128
D_IN_EVEN = 2 * D_A + D_B
D_IN_ODD = D_C + 2 * D_D
D_MIX = D_A + D_B
D_FF = ((8 * D_MODEL // 3 + 255) // 256) * 256
EPS = 1e-6

kernel_name = "hybrid_conv_pool_fourier_sgu_encoder"


def rmsnorm(x, g):
    xf = x.astype(jnp.float32)
    y = xf * lax.rsqrt(jnp.mean(xf * xf, axis=-1, keepdims=True) + EPS)
    return (y * g.astype(jnp.float32)).astype(x.dtype)


def layernorm(x, g, b):
    xf = x.astype(jnp.float32)
    mu = jnp.mean(xf, axis=-1, keepdims=True)
    var = jnp.mean(jnp.square(xf - mu), axis=-1, keepdims=True)
    y = (xf - mu) * lax.rsqrt(var + EPS)
    return (y * g.astype(jnp.float32) + b.astype(jnp.float32)).astype(x.dtype)


def conformer_conv(a, gate, conv_w, conv_b, ln_g, ln_b):
    u = a * jax.nn.sigmoid(gate)
    u = lax.conv_general_dilated(
        u, conv_w[:, None, :].astype(u.dtype), window_strides=(1,),
        padding=[(CONV_PAD, CONV_PAD)], dimension_numbers=('NWC', 'WIO', 'NWC'),
        feature_group_count=D_A) + conv_b
    return jax.nn.silu(layernorm(u, ln_g, ln_b))


def multiscale_pool(p, pool_w, pool_scale):
    bn, s, _ = p.shape
    y = p.reshape(bn, s, N_POOL, G_POOL).astype(jnp.float32)
    cs = jnp.concatenate([jnp.zeros((bn, 1, N_POOL, G_POOL), jnp.float32),
                          jnp.cumsum(y, axis=1)], axis=1)
    t = jnp.arange(s)[:, None]
    half = jnp.array(POOL_WINDOWS, dtype=jnp.int32)[None, :] // 2
    lo = jnp.clip(t - half, 0, s - 1)
    hi = jnp.clip(t + half - 1, 0, s - 1)
    gi = jnp.arange(N_POOL)[None, :]
    win_sum = cs[:, hi + 1, gi] - cs[:, lo, gi]
    count = (hi - lo + 1).astype(jnp.float32)[..., None]
    pooled = (win_sum / count - y).astype(p.dtype)
    mixed = jnp.einsum('bsgc,gcd->bsgd', pooled, pool_w) * pool_scale
    return mixed.reshape(bn, s, D_B)


def even_mixer(h, w_in, conv_w, conv_b, ln_g, ln_b, pool_w, pool_scale, w_out):
    z = h @ w_in
    a, gate, p = z[..., :D_A], z[..., D_A:2 * D_A], z[..., 2 * D_A:]
    ya = conformer_conv(a, gate, conv_w, conv_b, ln_g, ln_b)
    yb = multiscale_pool(p, pool_w, pool_scale)
    return jnp.concatenate([ya, yb], axis=-1) @ w_out


def fourier_mix(c, fourier_w):
    bn, s, _ = c.shape
    cf = c.reshape(bn, s, N_FOURIER, H_C).astype(jnp.float32)
    yc = jnp.fft.fft2(cf, axes=(1, 3), norm='ortho').real.astype(c.dtype)
    return jnp.einsum('bshc,hcd->bshd', yc, fourier_w).reshape(bn, s, D_C)


def spatial_gating(u, v, v_ln_g, v_ln_b, spatial_w, spatial_b):
    bn, s, _ = v.shape
    vn = layernorm(v.reshape(bn, s, N_SGU, H_D), v_ln_g, v_ln_b)
    vn = vn.reshape(bn, s // CHUNK, CHUNK, N_SGU, H_D)
    sv = jnp.einsum('hqk,bnkhc->bnqhc', spatial_w, vn) + spatial_b.T[:, :, None]
    return u * sv.reshape(bn, s, D_D)


def odd_mixer(h, w_in, fourier_w, v_ln_g, v_ln_b, spatial_w, spatial_b, w_out):
    z = h @ w_in
    c = z[..., :D_C]
    uv = jax.nn.gelu(z[..., D_C:], approximate=False)
    u, v = uv[..., :D_D], uv[..., D_D:]
    yc = fourier_mix(c, fourier_w)
    yd = spatial_gating(u, v, v_ln_g, v_ln_b, spatial_w, spatial_b)
    return jnp.concatenate([yc, yd], axis=-1) @ w_out


def swiglu(h, w_gate, w_up, w_down):
    return (jax.nn.silu(h @ w_gate) * (h @ w_up)) @ w_down


def setup_inputs(seed: int = 0) -> dict:
    key = jax.random.key(seed)
    ks = jax.random.split(key, 24)
    f32 = jnp.float32
    nrm = lambda k, shape, scale: jax.random.normal(k, shape, f32) * scale
    return {
        'x': jax.random.normal(ks[0], (BATCH, SEQ, D_MODEL), f32),
        'mix_norm_g': 1.0 + nrm(ks[1], (DEPTH, D_MODEL), 0.02),
        'ffn_norm_g': 1.0 + nrm(ks[2], (DEPTH, D_MODEL), 0.02),
        'ev_w_in': nrm(ks[3], (N_EVEN, D_MODEL, D_IN_EVEN), D_MODEL ** -0.5),
        'ev_conv_w': nrm(ks[4], (N_EVEN, CONV_WIDTH, D_A), CONV_WIDTH ** -0.5),
        'ev_conv_b': nrm(ks[5], (N_EVEN, D_A), 0.02),
        'ev_ln_g': 1.0 + nrm(ks[6], (N_EVEN, D_A), 0.02),
        'ev_ln_b': nrm(ks[7], (N_EVEN, D_A), 0.02),
        'ev_pool_w': nrm(ks[8], (N_EVEN, N_POOL, G_POOL, G_POOL), G_POOL ** -0.5),
        'ev_pool_scale': 1.0 + nrm(ks[9], (N_EVEN, N_POOL, G_POOL), 0.02),
        'ev_w_out': nrm(ks[10], (N_EVEN, D_MIX, D_MODEL), D_MIX ** -0.5),
        'od_w_in': nrm(ks[11], (N_ODD, D_MODEL, D_IN_ODD), D_MODEL ** -0.5),
        'od_fourier_w': nrm(ks[12], (N_ODD, N_FOURIER, H_C, H_C), H_C ** -0.5),
        'od_v_ln_g': 1.0 + nrm(ks[13], (N_ODD, N_SGU, H_D), 0.02),
        'od_v_ln_b': nrm(ks[14], (N_ODD, N_SGU, H_D), 0.02),
        'od_spatial_w': nrm(ks[15], (N_ODD, N_SGU, CHUNK, CHUNK), CHUNK ** -0.5),
        'od_spatial_b': 1.0 + nrm(ks[16], (N_ODD, N_SGU, CHUNK), 0.02),
        'od_w_out': nrm(ks[17], (N_ODD, D_MIX, D_MODEL), D_MIX ** -0.5),
        'ffn_w_gate': nrm(ks[18], (DEPTH, D_MODEL, D_FF), D_MODEL ** -0.5),
        'ffn_w_up': nrm(ks[19], (DEPTH, D_MODEL, D_FF), D_MODEL ** -0.5),
        'ffn_w_down': nrm(ks[20], (DEPTH, D_FF, D_MODEL), D_FF ** -0.5),
        'final_norm_g': 1.0 + nrm(ks[21], (D_MODEL,), 0.02),
    }


def reference(x, mix_norm_g, ffn_norm_g, ev_w_in, ev_conv_w, ev_conv_b, ev_ln_g,
              ev_ln_b, ev_pool_w, ev_pool_scale, ev_w_out, od_w_in, od_fourier_w,
              od_v_ln_g, od_v_ln_b, od_spatial_w, od_spatial_b, od_w_out,
              ffn_w_gate, ffn_w_up, ffn_w_down, final_norm_g):
    for l in range(DEPTH):
        h = rmsnorm(x, mix_norm_g[l])
        if l % 2 == 0:
            i = l // 2
            x = x + even_mixer(h, ev_w_in[i], ev_conv_w[i], ev_conv_b[i], ev_ln_g[i],
                               ev_ln_b[i], ev_pool_w[i], ev_pool_scale[i], ev_w_out[i])
        else:
            i = l // 2
            x = x + odd_mixer(h, od_w_in[i], od_fourier_w[i], od_v_ln_g[i], od_v_ln_b[i],
                              od_spatial_w[i], od_spatial_b[i], od_w_out[i])
        h = rmsnorm(x, ffn_norm_g[l])
        x = x + swiglu(h, ffn_w_gate[l], ffn_w_up[l], ffn_w_down[l])
    return rmsnorm(x, final_norm_g)
```

```python
import functools
import math

import numpy as np
import jax
import jax.numpy as jnp
from jax import lax
from jax.experimental import pallas as pl
from jax.experimental.pallas import tpu as pltpu

EPS = 1e-6
CONV_WIDTH = 31
CONV_PAD = CONV_WIDTH // 2
POOL_WINDOWS = (2, 4, 8, 16)
N_HEADS = 4
CHUNK = 128
HALO = 16
ROW_BLOCK = 64
MXU_DIM = 256
VMEM_LIMIT = 56 * 1024 * 1024

F32 = jnp.float32
BF16 = jnp.bfloat16


def _rms(xv, g):
    ms = jnp.mean(xv * xv, axis=-1, keepdims=True)
    return xv * lax.rsqrt(ms + EPS) * g


def _layernorm(xv, g, b):
    mu = jnp.mean(xv, axis=-1, keepdims=True)
    d = xv - mu
    var = jnp.mean(d * d, axis=-1, keepdims=True)
    return d * lax.rsqrt(var + EPS) * g + b


def _dot(a, b):
    return jnp.dot(a, b, preferred_element_type=F32)


def _const_spec(shape):
    nd = len(shape)
    return pl.BlockSpec(shape, lambda *_: (0,) * nd, pipeline_mode=pl.Buffered(1))


def _params(semantics):
    return pltpu.CompilerParams(dimension_semantics=semantics, vmem_limit_bytes=VMEM_LIMIT)


def _ffn_chunks(d_ff):
    step = 4 * MXU_DIM
    return tuple((c, min(c + step, d_ff)) for c in range(0, d_ff, step))


def _ffn_kernel(*refs, chunks, final):
    if final:
        x_ref, g_ref, wg_ref, wu_ref, wd_ref, fg_ref, o_ref = refs
    else:
        x_ref, g_ref, wg_ref, wu_ref, wd_ref, o_ref = refs
    x = x_ref[...]
    h = _rms(x, g_ref[...]).astype(BF16)
    acc = None
    for c0, c1 in chunks:
        gate = _dot(h, wg_ref[:, c0:c1])
        up = _dot(h, wu_ref[:, c0:c1])
        a = (gate * jax.nn.sigmoid(gate) * up).astype(BF16)
        d = _dot(a, wd_ref[c0:c1, :])
        acc = d if acc is None else acc + d
    y = x + acc
    if final:
        y = _rms(y, fg_ref[...])
    o_ref[...] = y


def _ffn_call(x2d, g, wg, wu, wd, final_g, tm):
    t, d = x2d.shape
    f = wg.shape[1]
    final = final_g is not None
    operands = [x2d, g.reshape(1, d), wg, wu, wd]
    in_specs = [pl.BlockSpec((tm, d), lambda i: (i, 0)), _const_spec((1, d)),
                _const_spec((d, f)), _const_spec((d, f)), _const_spec((f, d))]
    if final:
        operands.append(final_g.reshape(1, d))
        in_specs.append(_const_spec((1, d)))
    return pl.pallas_call(
        functools.partial(_ffn_kernel, chunks=_ffn_chunks(f), final=final),
        grid=(t // tm,),
        in_specs=in_specs,
        out_specs=pl.BlockSpec((tm, d), lambda i: (i, 0)),
        out_shape=jax.ShapeDtypeStruct((t, d), F32),
        compiler_params=_params(("parallel",)),
        name="ffn_final" if final else "ffn",
    )(*operands)


def _even_kernel(xm_ref, xp_ref, xn_ref, g_ref, win_ref, cw_ref, cb_ref, lg_ref, lb_ref,
                 pw_ref, ps_ref, wout_ref, o_ref, h_s, u_s, p_s, ya_s, pl_s, *, ts, seq):
    j = pl.program_id(1)
    da = u_s.shape[1]
    gp = da // N_HEADS
    g = g_ref[...]
    h_s[0:HALO, :] = _rms(xp_ref[...], g).astype(BF16)
    h_s[HALO:HALO + ts, :] = _rms(xm_ref[...], g).astype(BF16)
    h_s[HALO + ts:, :] = _rms(xn_ref[...], g).astype(BF16)
    z = _dot(h_s[...], win_ref[...])
    pos = j * ts - HALO + lax.broadcasted_iota(jnp.int32, (ts + 2 * HALO, 1), 0)
    valid = (pos >= 0) & (pos < seq)
    u_s[...] = jnp.where(valid, z[:, :da] * jax.nn.sigmoid(z[:, da:2 * da]), 0.0)
    p_s[...] = jnp.where(valid, z[:, 2 * da:], 0.0)

    for r0 in range(0, ts, ROW_BLOCK):
        acc = jnp.broadcast_to(cb_ref[...], (ROW_BLOCK, da))
        for k in range(CONV_WIDTH):
            acc = acc + u_s[pl.ds(r0 + HALO - CONV_PAD + k, ROW_BLOCK), :] * cw_ref[k:k + 1, :]
        ya_s[r0:r0 + ROW_BLOCK, :] = acc
        t = j * ts + r0 + lax.broadcasted_iota(jnp.int32, (ROW_BLOCK, 1), 0)
        for gi, w in enumerate(POOL_WINDOWS):
            half = w // 2
            lanes = slice(gi * gp, (gi + 1) * gp)
            s = p_s[pl.ds(r0 + HALO - half, ROW_BLOCK), lanes]
            for o in range(1, w):
                s = s + p_s[pl.ds(r0 + HALO - half + o, ROW_BLOCK), lanes]
            cnt = (jnp.minimum(t + half - 1, seq - 1) - jnp.maximum(t - half, 0) + 1).astype(F32)
            pl_s[r0:r0 + ROW_BLOCK, lanes] = s / cnt - p_s[pl.ds(r0 + HALO, ROW_BLOCK), lanes]

    ln = _layernorm(ya_s[...], lg_ref[...], lb_ref[...])
    ya = (ln * jax.nn.sigmoid(ln)).astype(BF16)
    yb = (_dot(pl_s[...].astype(BF16), pw_ref[...]) * ps_ref[...]).astype(BF16)
    y = jnp.concatenate([ya, yb], axis=-1)
    o_ref[...] = xm_ref[...] + _dot(y, wout_ref[...])


def _even_call(x, g, w_in, conv_w, conv_b, ln_g, ln_b, pool_bd, pool_scale, w_out, ts):
    b, s, d = x.shape
    da = conv_w.shape[1]
    nh = ts // HALO
    kernel = functools.partial(_even_kernel, ts=ts, seq=s)
    return pl.pallas_call(
        kernel,
        grid=(b, s // ts),
        in_specs=[
            pl.BlockSpec((None, ts, d), lambda bi, j: (bi, j, 0)),
            pl.BlockSpec((None, HALO, d), lambda bi, j: (bi, jnp.maximum(j * nh - 1, 0), 0)),
            pl.BlockSpec((None, HALO, d), lambda bi, j: (bi, jnp.minimum((j + 1) * nh, s // HALO - 1), 0)),
            _const_spec((1, d)), _const_spec(w_in.shape), _const_spec(conv_w.shape),
            _const_spec((1, da)), _const_spec((1, da)), _const_spec((1, da)),
            _const_spec(pool_bd.shape), _const_spec((1, da)), _const_spec(w_out.shape),
        ],
        out_specs=pl.BlockSpec((None, ts, d), lambda bi, j: (bi, j, 0)),
        out_shape=jax.ShapeDtypeStruct((b, s, d), F32),
        scratch_shapes=[
            pltpu.VMEM((ts + 2 * HALO, d), BF16),
            pltpu.VMEM((ts + 2 * HALO, da), F32),
            pltpu.VMEM((ts + 2 * HALO, da), F32),
            pltpu.VMEM((ts, da), F32),
            pltpu.VMEM((ts, da), F32),
        ],
        compiler_params=_params(("parallel", "parallel")),
        name="even_mixer",
    )(x, x, x, g.reshape(1, d), w_in, conv_w, conv_b.reshape(1, da), ln_g.reshape(1, da),
      ln_b.reshape(1, da), pool_bd, pool_scale.reshape(1, da), w_out)


def _odd_in_kernel(x_ref, g_ref, win_ref, cc_ref, sc_ref, vg_ref, vb_ref, sw_ref, sb_ref,
                   ab_ref, yd_ref, *, ts):
    dc = cc_ref.shape[0]
    hd = dc // N_HEADS
    h = _rms(x_ref[...], g_ref[...]).astype(BF16)
    z = _dot(h, win_ref[...])
    c = z[:, :dc].astype(BF16)
    ab_ref[0] = _dot(c, cc_ref[...]).astype(BF16)
    ab_ref[1] = _dot(c, sc_ref[...]).astype(BF16)
    zuv = z[:, dc:]
    uv = 0.5 * zuv * (1.0 + lax.erf(zuv * math.sqrt(0.5)))
    u = uv[:, :dc]
    v = uv[:, dc:]
    for hi in range(N_HEADS):
        lanes = slice(hi * hd, (hi + 1) * hd)
        vn = _layernorm(v[:, lanes], vg_ref[:, lanes], vb_ref[:, lanes]).astype(BF16)
        for n in range(ts // CHUNK):
            rows = slice(n * CHUNK, (n + 1) * CHUNK)
            sv = _dot(sw_ref[hi], vn[rows, :]) + sb_ref[:, lanes]
            yd_ref[rows, lanes] = (u[rows, lanes] * sv).astype(BF16)


def _odd_in_call(x, g, w_in, cc_bd, sc_bd, v_ln_g, v_ln_b, spatial_w, sb_full, ts):
    b, s, d = x.shape
    dc = cc_bd.shape[0]
    kernel = functools.partial(_odd_in_kernel, ts=ts)
    return pl.pallas_call(
        kernel,
        grid=(b, s // ts),
        in_specs=[
            pl.BlockSpec((None, ts, d), lambda bi, j: (bi, j, 0)),
            _const_spec((1, d)), _const_spec(w_in.shape), _const_spec(cc_bd.shape),
            _const_spec(sc_bd.shape), _const_spec((1, dc)), _const_spec((1, dc)),
            _const_spec(spatial_w.shape), _const_spec(sb_full.shape),
        ],
        out_specs=[
            pl.BlockSpec((2, ts, dc), lambda bi, j: (0, j, bi)),
            pl.BlockSpec((None, ts, dc), lambda bi, j: (bi, j, 0)),
        ],
        out_shape=[
            jax.ShapeDtypeStruct((2, s, b * dc), BF16),
            jax.ShapeDtypeStruct((b, s, dc), BF16),
        ],
        compiler_params=_params(("parallel", "parallel")),
        name="odd_mixer_in",
    )(x, g.reshape(1, d), w_in, cc_bd, sc_bd, v_ln_g.reshape(1, dc), v_ln_b.reshape(1, dc),
      spatial_w, sb_full)


def _dft_kernel(cm_ref, abe_ref, abo_ref, o_ref, acc_e, acc_o, *, scale):
    k = pl.program_id(2)

    @pl.when(k == 0)
    def _():
        acc_e[...] = jnp.zeros_like(acc_e)
        acc_o[...] = jnp.zeros_like(acc_o)

    acc_e[...] += _dot(cm_ref[0, 0], abe_ref[0]) + _dot(cm_ref[0, 1], abe_ref[1])
    acc_o[...] += _dot(cm_ref[1, 0], abo_ref[0]) + _dot(cm_ref[1, 1], abo_ref[1])

    @pl.when(k == pl.num_programs(2) - 1)
    def _():
        e = acc_e[...]
        o = acc_o[...]
        o_ref[0] = ((e + o) * scale).astype(o_ref.dtype)
        o_ref[1] = ((e - o) * scale).astype(o_ref.dtype)


def _dft_call(cmat, ab, scale, tm, tn, tk):
    _, s, nc = ab.shape
    half = s // 2
    ab2 = ab.reshape(2, half, 2 * nc)
    kernel = functools.partial(_dft_kernel, scale=scale)
    out = pl.pallas_call(
        kernel,
        grid=(half // tm, nc // tn, half // tk),
        in_specs=[
            pl.BlockSpec((2, 2, tm, tk), lambda i, n, k: (0, 0, i, k)),
            pl.BlockSpec((2, tk, tn), lambda i, n, k: (0, k, n)),
            pl.BlockSpec((2, tk, tn), lambda i, n, k: (0, k, nc // tn + n)),
        ],
        out_specs=pl.BlockSpec((2, tm, tn), lambda i, n, k: (0, i, n)),
        out_shape=jax.ShapeDtypeStruct((2, half, nc), BF16),
        scratch_shapes=[pltpu.VMEM((tm, tn), F32), pltpu.VMEM((tm, tn), F32)],
        compiler_params=_params(("parallel", "parallel", "arbitrary")),
        name="seq_dft",
    )(cmat, ab2, ab2)
    return out.reshape(s, nc)


def _odd_out_kernel(x_ref, yp_ref, yd_ref, fw_ref, wout_ref, o_ref):
    yc = _dot(yp_ref[...], fw_ref[...]).astype(BF16)
    y = jnp.concatenate([yc, yd_ref[...]], axis=-1)
    o_ref[...] = x_ref[...] + _dot(y, wout_ref[...])


def _odd_out_call(x, ypre, yd, fw_bd, w_out, ts):
    b, s, d = x.shape
    dc = fw_bd.shape[0]
    return pl.pallas_call(
        _odd_out_kernel,
        grid=(b, s // ts),
        in_specs=[
            pl.BlockSpec((None, ts, d), lambda bi, j: (bi, j, 0)),
            pl.BlockSpec((ts, dc), lambda bi, j: (j, bi)),
            pl.BlockSpec((None, ts, dc), lambda bi, j: (bi, j, 0)),
            _const_spec(fw_bd.shape), _const_spec(w_out.shape),
        ],
        out_specs=pl.BlockSpec((None, ts, d), lambda bi, j: (bi, j, 0)),
        out_shape=jax.ShapeDtypeStruct((b, s, d), F32),
        compiler_params=_params(("parallel", "parallel")),
        name="odd_mixer_out",
    )(x, ypre, yd, fw_bd, w_out)


def _block_diag(w):
    g, n, _ = w.shape
    eye = jnp.eye(g, dtype=w.dtype)
    return (eye[:, None, :, None] * w[:, :, None, :]).reshape(g * n, g * n)


def _channel_dft_mats(n):
    idx = np.outer(np.arange(n), np.arange(n)) % n
    ang = 2.0 * np.pi * idx / n
    return np.cos(ang).astype(np.float32), np.sin(ang).astype(np.float32)


def _seq_dft_mats(s):
    half = s // 2
    j = lax.broadcasted_iota(jnp.int32, (half, half), 0)
    m = lax.broadcasted_iota(jnp.int32, (half, half), 1)
    parts = []
    for parity in (0, 1):
        idx = (j * (2 * m + parity)) % s
        ang = idx.astype(F32) * (2.0 * math.pi / s)
        parts.append(jnp.stack([jnp.cos(ang), -jnp.sin(ang)]))
    return jnp.stack(parts).astype(BF16)


def kernel(x, mix_norm_g, ffn_norm_g, ev_w_in, ev_conv_w, ev_conv_b, ev_ln_g, ev_ln_b, ev_pool_w,
           ev_pool_scale, ev_w_out, od_w_in, od_fourier_w, od_v_ln_g, od_v_ln_b, od_spatial_w,
           od_spatial_b, od_w_out, ffn_w_gate, ffn_w_up, ffn_w_down, final_norm_g):
    b, s, d = x.shape
    depth = mix_norm_g.shape[0]
    hc = od_fourier_w.shape[2]
    ts = 512
    tm = 1024

    cc, sc = _channel_dft_mats(hc)
    cc_bd = _block_diag(jnp.asarray(np.broadcast_to(cc, (N_HEADS, hc, hc)))).astype(BF16)
    sc_bd = _block_diag(jnp.asarray(np.broadcast_to(sc, (N_HEADS, hc, hc)))).astype(BF16)
    cmat = _seq_dft_mats(s)
    dft_scale = 1.0 / math.sqrt(s * hc)

    for l in range(depth):
        i = l // 2
        if l % 2 == 0:
            x = _even_call(x, mix_norm_g[l], ev_w_in[i].astype(BF16), ev_conv_w[i], ev_conv_b[i],
                           ev_ln_g[i], ev_ln_b[i], _block_diag(ev_pool_w[i]).astype(BF16),
                           ev_pool_scale[i].reshape(-1), ev_w_out[i].astype(BF16), ts)
        else:
            sb_full = jnp.broadcast_to(od_spatial_b[i].T[:, :, None], (CHUNK, N_HEADS, hc))
            ab, yd = _odd_in_call(x, mix_norm_g[l], od_w_in[i].astype(BF16), cc_bd, sc_bd,
                                  od_v_ln_g[i].reshape(-1), od_v_ln_b[i].reshape(-1),
                                  od_spatial_w[i].astype(BF16), sb_full.reshape(CHUNK, N_HEADS * hc), ts)
            ypre = _dft_call(cmat, ab, dft_scale, 1024, 1024, 512)
            x = _odd_out_call(x, ypre, yd, _block_diag(od_fourier_w[i]).astype(BF16),
                              od_w_out[i].astype(BF16), ts)
        final_g = final_norm_g if l == depth - 1 else None
        x = _ffn_call(x.reshape(b * s, d), ffn_norm_g[l], ffn_w_gate[l].astype(BF16),
                      ffn_w_up[l].astype(BF16), ffn_w_down[l].astype(BF16), final_g, tm).reshape(b, s, d)
    return x
```

```python
import functools
import math

import numpy as np
import jax
import jax.numpy as jnp
from jax import lax
from jax.experimental import pallas as pl
from jax.experimental.pallas import tpu as pltpu

EPS = 1e-6
CONV_WIDTH = 31
CONV_PAD = CONV_WIDTH // 2
POOL_WINDOWS = (2, 4, 8, 16)
N_HEADS = 4
CHUNK = 128
HALO = 16
ROW_BLOCK = 64
LANE_BLOCK = 256
SUBLANES = 8
LANES = 128
MXU_DIM = 256
VMEM_LIMIT = 56 * 1024 * 1024

F32 = jnp.float32
BF16 = jnp.bfloat16


def _rms(xv, g):
    ms = jnp.mean(xv * xv, axis=-1, keepdims=True)
    return xv * lax.rsqrt(ms + EPS) * g


def _layernorm(xv, g, b):
    mu = jnp.mean(xv, axis=-1, keepdims=True)
    d = xv - mu
    var = jnp.mean(d * d, axis=-1, keepdims=True)
    return d * lax.rsqrt(var + EPS) * g + b


def _dot(a, b):
    return jnp.dot(a, b, preferred_element_type=F32)


def _const_spec(shape):
    nd = len(shape)
    return pl.BlockSpec(shape, lambda *_: (0,) * nd, pipeline_mode=pl.Buffered(1))


def _params(semantics):
    return pltpu.CompilerParams(dimension_semantics=semantics, vmem_limit_bytes=VMEM_LIMIT)


def _ffn_chunks(d_ff):
    step = 4 * MXU_DIM
    return tuple((c, min(c + step, d_ff)) for c in range(0, d_ff, step))


def _ffn_kernel(*refs, chunks, final):
    if final:
        x_ref, g_ref, wg_ref, wu_ref, wd_ref, fg_ref, o_ref = refs
    else:
        x_ref, g_ref, wg_ref, wu_ref, wd_ref, o_ref = refs
    x = x_ref[...]
    h = _rms(x, g_ref[...]).astype(BF16)
    acc = None
    for c0, c1 in chunks:
        gate = _dot(h, wg_ref[:, c0:c1])
        up = _dot(h, wu_ref[:, c0:c1])
        a = (gate * jax.nn.sigmoid(gate) * up).astype(BF16)
        d = _dot(a, wd_ref[c0:c1, :])
        acc = d if acc is None else acc + d
    y = x + acc
    if final:
        y = _rms(y, fg_ref[...])
    o_ref[...] = y


def _ffn_call(x2d, g, wg, wu, wd, final_g, tm):
    t, d = x2d.shape
    f = wg.shape[1]
    final = final_g is not None
    operands = [x2d, g.reshape(1, d), wg, wu, wd]
    in_specs = [pl.BlockSpec((tm, d), lambda i: (i, 0)), _const_spec((1, d)),
                _const_spec((d, f)), _const_spec((d, f)), _const_spec((f, d))]
    if final:
        operands.append(final_g.reshape(1, d))
        in_specs.append(_const_spec((1, d)))
    return pl.pallas_call(
        functools.partial(_ffn_kernel, chunks=_ffn_chunks(f), final=final),
        grid=(t // tm,),
        in_specs=in_specs,
        out_specs=pl.BlockSpec((tm, d), lambda i: (i, 0)),
        out_shape=jax.ShapeDtypeStruct((t, d), F32),
        compiler_params=_params(("parallel",)),
        name="ffn_final" if final else "ffn",
    )(*operands)


def _shift_rows(a, k):
    return pltpu.roll(a, a.shape[0] - k, axis=0) if k else a


def _even_kernel(xm_ref, xp_ref, xn_ref, g_ref, win_ref, cw_ref, cb_ref, lg_ref, lb_ref,
                 pw_ref, ps_ref, wout_ref, o_ref, h_s, u_s, p_s, y_s, pl_s, *, ts, seq):
    j = pl.program_id(1)
    da = u_s.shape[1]
    gp = da // N_HEADS
    g = g_ref[...]
    h_s[0:HALO, :] = _rms(xp_ref[...], g).astype(BF16)
    h_s[HALO:HALO + ts, :] = _rms(xm_ref[...], g).astype(BF16)
    h_s[HALO + ts:, :] = _rms(xn_ref[...], g).astype(BF16)
    z = _dot(h_s[...], win_ref[...])
    pos = j * ts - HALO + lax.broadcasted_iota(jnp.int32, (ts + 2 * HALO, 1), 0)
    valid = (pos >= 0) & (pos < seq)
    u_s[...] = jnp.where(valid, z[:, :da] * jax.nn.sigmoid(z[:, da:2 * da]), 0.0)
    p_s[...] = jnp.where(valid, z[:, 2 * da:], 0.0)

    def row_block(rb, carry):
        r0 = pl.multiple_of(rb * ROW_BLOCK, ROW_BLOCK)
        parts = []
        for l0 in range(0, da, LANE_BLOCK):
            lanes = slice(l0, l0 + LANE_BLOCK)
            out = jnp.broadcast_to(cb_ref[:, lanes], (ROW_BLOCK, LANE_BLOCK))
            for r in range(SUBLANES):
                v = None
                for q in range(pl.cdiv(CONV_WIDTH + HALO - CONV_PAD, SUBLANES)):
                    k = SUBLANES * q + r - (HALO - CONV_PAD)
                    if 0 <= k < CONV_WIDTH:
                        term = (u_s[pl.ds(r0 + SUBLANES * q, ROW_BLOCK + SUBLANES), lanes]
                                * cw_ref[k:k + 1, lanes])
                        v = term if v is None else v + term
                out = out + _shift_rows(v, r)[:ROW_BLOCK]
            parts.append(out)
        ln = _layernorm(jnp.concatenate(parts, axis=-1), lg_ref[...], lb_ref[...])
        y_s[pl.ds(r0, ROW_BLOCK), 0:da] = (ln * jax.nn.sigmoid(ln)).astype(BF16)
        t = j * ts + r0 + lax.broadcasted_iota(jnp.int32, (ROW_BLOCK, 1), 0)
        for gi, w in enumerate(POOL_WINDOWS):
            half = w // 2
            lanes = slice(gi * gp, (gi + 1) * gp)
            xw = p_s[pl.ds(r0 + HALO - SUBLANES, ROW_BLOCK + 2 * SUBLANES), lanes]
            d = xw
            size = 1
            while size < half:
                d = d + _shift_rows(d, size)
                size *= 2
            s = _shift_rows(d, SUBLANES - half)[:ROW_BLOCK] + d[SUBLANES:SUBLANES + ROW_BLOCK]
            cnt = (jnp.minimum(t + half - 1, seq - 1) - jnp.maximum(t - half, 0) + 1).astype(F32)
            pooled = s / cnt - xw[SUBLANES:SUBLANES + ROW_BLOCK]
            pl_s[pl.ds(r0, ROW_BLOCK), lanes] = pooled.astype(BF16)
        return carry

    lax.fori_loop(0, ts // ROW_BLOCK, row_block, 0)
    y_s[:, da:] = (_dot(pl_s[...], pw_ref[...]) * ps_ref[...]).astype(BF16)
    o_ref[...] = xm_ref[...] + _dot(y_s[...], wout_ref[...])


def _even_call(x, g, w_in, conv_w, conv_b, ln_g, ln_b, pool_bd, pool_scale, w_out, ts):
    b, s, d = x.shape
    da = conv_w.shape[1]
    nh = ts // HALO
    kernel = functools.partial(_even_kernel, ts=ts, seq=s)
    return pl.pallas_call(
        kernel,
        grid=(b, s // ts),
        in_specs=[
            pl.BlockSpec((None, ts, d), lambda bi, j: (bi, j, 0)),
            pl.BlockSpec((None, HALO, d), lambda bi, j: (bi, jnp.maximum(j * nh - 1, 0), 0)),
            pl.BlockSpec((None, HALO, d), lambda bi, j: (bi, jnp.minimum((j + 1) * nh, s // HALO - 1), 0)),
            _const_spec((1, d)), _const_spec(w_in.shape), _const_spec(conv_w.shape),
            _const_spec((1, da)), _const_spec((1, da)), _const_spec((1, da)),
            _const_spec(pool_bd.shape), _const_spec((1, da)), _const_spec(w_out.shape),
        ],
        out_specs=pl.BlockSpec((None, ts, d), lambda bi, j: (bi, j, 0)),
        out_shape=jax.ShapeDtypeStruct((b, s, d), F32),
        scratch_shapes=[
            pltpu.VMEM((ts + 2 * HALO, d), BF16),
            pltpu.VMEM((ts + 2 * HALO, da), F32),
            pltpu.VMEM((ts + 2 * HALO, da), F32),
            pltpu.VMEM((ts, d), BF16),
            pltpu.VMEM((ts, da), BF16),
        ],
        compiler_params=_params(("parallel", "parallel")),
        name="even_mixer",
    )(x, x, x, g.reshape(1, d), w_in, conv_w, conv_b.reshape(1, da), ln_g.reshape(1, da),
      ln_b.reshape(1, da), pool_bd, pool_scale.reshape(1, da), w_out)


def _odd_in_kernel(x_ref, g_ref, win_ref, cc_ref, sc_ref, vg_ref, vb_ref, sw_ref, sb_ref,
                   ab_ref, yd_ref, ab_s, *, ts):
    dc = cc_ref.shape[0]
    hd = dc // N_HEADS
    h = _rms(x_ref[...], g_ref[...]).astype(BF16)
    z = _dot(h, win_ref[...])
    c = z[:, :dc].astype(BF16)
    for q, m_ref in enumerate((cc_ref, sc_ref)):
        part = _dot(c, m_ref[...])
        for blk in range(dc // LANES):
            lanes = slice(blk * LANES, (blk + 1) * LANES)
            ab_s[blk] = part[:, lanes]
            for parity in range(2):
                ab_ref[q, parity, :, lanes] = ab_s[blk, pl.ds(parity, ts // 2, stride=2), :].astype(BF16)
    zuv = z[:, dc:]
    uv = 0.5 * zuv * (1.0 + lax.erf(zuv * math.sqrt(0.5)))
    u = uv[:, :dc]
    v = uv[:, dc:]
    for hi in range(N_HEADS):
        lanes = slice(hi * hd, (hi + 1) * hd)
        vn = _layernorm(v[:, lanes], vg_ref[:, lanes], vb_ref[:, lanes]).astype(BF16)
        for n in range(ts // CHUNK):
            rows = slice(n * CHUNK, (n + 1) * CHUNK)
            sv = _dot(sw_ref[hi], vn[rows, :]) + sb_ref[:, lanes]
            yd_ref[rows, lanes] = (u[rows, lanes] * sv).astype(BF16)


def _odd_in_call(x, g, w_in, cc_bd, sc_bd, v_ln_g, v_ln_b, spatial_w, sb_full, ts):
    b, s, d = x.shape
    dc = cc_bd.shape[0]
    kernel = functools.partial(_odd_in_kernel, ts=ts)
    return pl.pallas_call(
        kernel,
        grid=(b, s // ts),
        in_specs=[
            pl.BlockSpec((None, ts, d), lambda bi, j: (bi, j, 0)),
            _const_spec((1, d)), _const_spec(w_in.shape), _const_spec(cc_bd.shape),
            _const_spec(sc_bd.shape), _const_spec((1, dc)), _const_spec((1, dc)),
            _const_spec(spatial_w.shape), _const_spec(sb_full.shape),
        ],
        out_specs=[
            pl.BlockSpec((2, 2, ts // 2, dc), lambda bi, j: (0, 0, j, bi)),
            pl.BlockSpec((None, ts, dc), lambda bi, j: (bi, j, 0)),
        ],
        out_shape=[
            jax.ShapeDtypeStruct((2, 2, s // 2, b * dc), BF16),
            jax.ShapeDtypeStruct((b, s, dc), BF16),
        ],
        scratch_shapes=[pltpu.VMEM((dc // LANES, ts, LANES), F32)],
        compiler_params=_params(("parallel", "parallel")),
        name="odd_mixer_in",
    )(x, g.reshape(1, d), w_in, cc_bd, sc_bd, v_ln_g.reshape(1, dc), v_ln_b.reshape(1, dc),
      spatial_w, sb_full)


def _dft_kernel(cm_ref, ab_ref, o_ref, acc_e, acc_o, *, scale):
    k = pl.program_id(2)

    @pl.when(k == 0)
    def _():
        acc_e[...] = jnp.zeros_like(acc_e)
        acc_o[...] = jnp.zeros_like(acc_o)

    acc_e[...] += _dot(cm_ref[0, 0], ab_ref[0, 0]) + _dot(cm_ref[0, 1], ab_ref[1, 0])
    acc_o[...] += _dot(cm_ref[1, 0], ab_ref[0, 1]) + _dot(cm_ref[1, 1], ab_ref[1, 1])

    @pl.when(k == pl.num_programs(2) - 1)
    def _():
        e = acc_e[...]
        o = acc_o[...]
        o_ref[0] = ((e + o) * scale).astype(o_ref.dtype)
        o_ref[1] = ((e - o) * scale).astype(o_ref.dtype)


def _dft_call(cmat, ab, scale, tm, tn, tk):
    _, _, half, nc = ab.shape
    kernel = functools.partial(_dft_kernel, scale=scale)
    out = pl.pallas_call(
        kernel,
        grid=(half // tm, nc // tn, half // tk),
        in_specs=[
            pl.BlockSpec((2, 2, tm, tk), lambda i, n, k: (0, 0, i, k)),
            pl.BlockSpec((2, 2, tk, tn), lambda i, n, k: (0, 0, k, n)),
        ],
        out_specs=pl.BlockSpec((2, tm, tn), lambda i, n, k: (0, i, n)),
        out_shape=jax.ShapeDtypeStruct((2, half, nc), BF16),
        scratch_shapes=[pltpu.VMEM((tm, tn), F32), pltpu.VMEM((tm, tn), F32)],
        compiler_params=_params(("parallel", "parallel", "arbitrary")),
        name="seq_dft",
    )(cmat, ab)
    return out.reshape(2 * half, nc)


def _odd_out_kernel(x_ref, yp_ref, yd_ref, fw_ref, wout_ref, o_ref):
    yc = _dot(yp_ref[...], fw_ref[...]).astype(BF16)
    y = jnp.concatenate([yc, yd_ref[...]], axis=-1)
    o_ref[...] = x_ref[...] + _dot(y, wout_ref[...])


def _odd_out_call(x, ypre, yd, fw_bd, w_out, ts):
    b, s, d = x.shape
    dc = fw_bd.shape[0]
    return pl.pallas_call(
        _odd_out_kernel,
        grid=(b, s // ts),
        in_specs=[
            pl.BlockSpec((None, ts, d), lambda bi, j: (bi, j, 0)),
            pl.BlockSpec((ts, dc), lambda bi, j: (j, bi)),
            pl.BlockSpec((None, ts, dc), lambda bi, j: (bi, j, 0)),
            _const_spec(fw_bd.shape), _const_spec(w_out.shape),
        ],
        out_specs=pl.BlockSpec((None, ts, d), lambda bi, j: (bi, j, 0)),
        out_shape=jax.ShapeDtypeStruct((b, s, d), F32),
        compiler_params=_params(("parallel", "parallel")),
        name="odd_mixer_out",
    )(x, ypre, yd, fw_bd, w_out)


def _block_diag(w):
    g, n, _ = w.shape
    eye = jnp.eye(g, dtype=w.dtype)
    return (eye[:, None, :, None] * w[:, :, None, :]).reshape(g * n, g * n)


def _channel_dft_mats(n):
    idx = np.outer(np.arange(n), np.arange(n)) % n
    ang = 2.0 * np.pi * idx / n
    return np.cos(ang).astype(np.float32), np.sin(ang).astype(np.float32)


def _twiddle_kernel(ta_ref, tb_ref, o_ref):
    for blk in range(o_ref.shape[3] // LANES):
        ca = ta_ref[0, :, blk:blk + 1]
        sa = ta_ref[1, :, blk:blk + 1]
        cols = slice(blk * LANES, (blk + 1) * LANES)
        for parity in range(2):
            cb = tb_ref[parity, 0]
            sb = tb_ref[parity, 1]
            o_ref[parity, 0, :, cols] = (ca * cb - sa * sb).astype(o_ref.dtype)
            o_ref[parity, 1, :, cols] = (-(sa * cb) - ca * sb).astype(o_ref.dtype)


def _seq_dft_mats(s, rows):
    half = s // 2
    nblk = half // LANES
    unit = 2.0 * math.pi / s

    def cos_sin(idx):
        ang = (idx % s).astype(F32) * unit
        return jnp.stack([jnp.cos(ang), jnp.sin(ang)])

    j = lax.broadcasted_iota(jnp.int32, (half, nblk), 0)
    blk = lax.broadcasted_iota(jnp.int32, (half, nblk), 1)
    ta = cos_sin(j * (2 * LANES * blk))
    j = lax.broadcasted_iota(jnp.int32, (half, LANES), 0)
    ml = lax.broadcasted_iota(jnp.int32, (half, LANES), 1)
    tb = jnp.stack([cos_sin(j * (2 * ml + parity)) for parity in (0, 1)])
    return pl.pallas_call(
        _twiddle_kernel,
        grid=(half // rows,),
        in_specs=[pl.BlockSpec((2, rows, nblk), lambda i: (0, i, 0)),
                  pl.BlockSpec((2, 2, rows, LANES), lambda i: (0, 0, i, 0))],
        out_specs=pl.BlockSpec((2, 2, rows, half), lambda i: (0, 0, i, 0)),
        out_shape=jax.ShapeDtypeStruct((2, 2, half, half), BF16),
        compiler_params=_params(("parallel",)),
        name="seq_dft_twiddles",
    )(ta, tb)


def kernel(x, mix_norm_g, ffn_norm_g, ev_w_in, ev_conv_w, ev_conv_b, ev_ln_g, ev_ln_b, ev_pool_w,
           ev_pool_scale, ev_w_out, od_w_in, od_fourier_w, od_v_ln_g, od_v_ln_b, od_spatial_w,
           od_spatial_b, od_w_out, ffn_w_gate, ffn_w_up, ffn_w_down, final_norm_g):
    b, s, d = x.shape
    depth = mix_norm_g.shape[0]
    hc = od_fourier_w.shape[2]
    ts = 512
    tm = 1024

    cc, sc = _channel_dft_mats(hc)
    cc_bd = _block_diag(jnp.asarray(np.broadcast_to(cc, (N_HEADS, hc, hc)))).astype(BF16)
    sc_bd = _block_diag(jnp.asarray(np.broadcast_to(sc, (N_HEADS, hc, hc)))).astype(BF16)
    cmat = _seq_dft_mats(s, 256)
    dft_scale = 1.0 / math.sqrt(s * hc)

    for l in range(depth):
        i = l // 2
        if l % 2 == 0:
            x = _even_call(x, mix_norm_g[l], ev_w_in[i].astype(BF16), ev_conv_w[i], ev_conv_b[i],
                           ev_ln_g[i], ev_ln_b[i], _block_diag(ev_pool_w[i]).astype(BF16),
                           ev_pool_scale[i].reshape(-1), ev_w_out[i].astype(BF16), ts)
        else:
            sb_full = jnp.broadcast_to(od_spatial_b[i].T[:, :, None], (CHUNK, N_HEADS, hc))
            ab, yd = _odd_in_call(x, mix_norm_g[l], od_w_in[i].astype(BF16), cc_bd, sc_bd,
                                  od_v_ln_g[i].reshape(-1), od_v_ln_b[i].reshape(-1),
                                  od_spatial_w[i].astype(BF16), sb_full.reshape(CHUNK, N_HEADS * hc), ts)
            ypre = _dft_call(cmat, ab, dft_scale, 1024, 1024, 512)
            x = _odd_out_call(x, ypre, yd, _block_diag(od_fourier_w[i]).astype(BF16),
                              od_w_out[i].astype(BF16), ts)
        final_g = final_norm_g if l == depth - 1 else None
        x = _ffn_call(x.reshape(b * s, d), ffn_norm_g[l], ffn_w_gate[l].astype(BF16),
                      ffn_w_up[l].astype(BF16), ffn_w_down[l].astype(BF16), final_g, tm).reshape(b, s, d)
    return x
```

```python
import functools
import math

import numpy as np
import jax
import jax.numpy as jnp
from jax import lax
from jax.experimental import pallas as pl
from jax.experimental.pallas import tpu as pltpu

EPS = 1e-6
CONV_WIDTH = 31
CONV_PAD = CONV_WIDTH // 2
POOL_WINDOWS = (2, 4, 8, 16)
N_HEADS = 4
CHUNK = 128
HALO = 16
ROW_BLOCK = 64
LANE_BLOCK = 128
SUBLANES = 8
LANES = 128
MXU_DIM = 256
VMEM_LIMIT = 56 * 1024 * 1024

F32 = jnp.float32
BF16 = jnp.bfloat16


def _rms(xv, g):
    ms = jnp.mean(xv * xv, axis=-1, keepdims=True)
    return xv * lax.rsqrt(ms + EPS) * g


def _layernorm(xv, g, b):
    mu = jnp.mean(xv, axis=-1, keepdims=True)
    d = xv - mu
    var = jnp.mean(d * d, axis=-1, keepdims=True)
    return d * lax.rsqrt(var + EPS) * g + b


def _dot(a, b):
    return jnp.dot(a, b, preferred_element_type=F32)


def _const_spec(shape):
    nd = len(shape)
    return pl.BlockSpec(shape, lambda *_: (0,) * nd, pipeline_mode=pl.Buffered(1))


def _params(semantics, flags=None):
    return pltpu.CompilerParams(dimension_semantics=semantics, vmem_limit_bytes=VMEM_LIMIT, flags=flags)


def _shift_rows(a, k):
    return pltpu.roll(a, a.shape[0] - k, axis=0) if k else a


def _ffn_chunks(d_ff):
    step = 4 * MXU_DIM
    return tuple((c, min(c + step, d_ff)) for c in range(0, d_ff, step))


def _ffn_in(x, g_ref):
    return _rms(x, g_ref[...]).astype(BF16)


def _ffn(x, h, wg_ref, wu_ref, wd_ref, between=lambda: None):
    acc = x
    for c0, c1 in _ffn_chunks(wg_ref.shape[1]):
        gate = _dot(h, wg_ref[:, c0:c1])
        between()
        up = _dot(h, wu_ref[:, c0:c1])
        between()
        a = (gate * jax.nn.sigmoid(gate) * up).astype(BF16)
        acc = acc + _dot(a, wd_ref[c0:c1, :])
        between()
    return acc


def _even_front(j, xm_ref, xp_ref, xn_ref, g_ref, win_ref, h_s, u_s, p_s, *, ts, seq):
    da = u_s.shape[1]
    g = g_ref[...]
    h_s[0:HALO, :] = _rms(xp_ref[...], g).astype(BF16)
    h_s[HALO:HALO + ts, :] = _rms(xm_ref[...], g).astype(BF16)
    h_s[HALO + ts:, :] = _rms(xn_ref[...], g).astype(BF16)
    z = _dot(h_s[...], win_ref[...])
    pos = j * ts - HALO + lax.broadcasted_iota(jnp.int32, (ts + 2 * HALO, 1), 0)
    valid = (pos >= 0) & (pos < seq)
    u_s[...] = jnp.where(valid, z[:, :da] * jax.nn.sigmoid(z[:, da:2 * da]), 0.0)
    p_s[...] = jnp.where(valid, z[:, 2 * da:], 0.0)


def _even_rows(j, r0, cw_ref, cb_ref, lg_ref, lb_ref, u_s, p_s, y_s, pl_s, *, ts, seq):
    da = u_s.shape[1]
    gp = da // N_HEADS
    parts = []
    for l0 in range(0, da, LANE_BLOCK):
        lanes = slice(l0, l0 + LANE_BLOCK)
        out = jnp.broadcast_to(cb_ref[:, lanes], (ROW_BLOCK, LANE_BLOCK))
        for r in range(SUBLANES):
            v = None
            for q in range(pl.cdiv(CONV_WIDTH + HALO - CONV_PAD, SUBLANES)):
                k = SUBLANES * q + r - (HALO - CONV_PAD)
                if 0 <= k < CONV_WIDTH:
                    term = (u_s[r0 + SUBLANES * q:r0 + SUBLANES * q + ROW_BLOCK + SUBLANES, lanes]
                            * cw_ref[k:k + 1, lanes])
                    v = term if v is None else v + term
            out = out + _shift_rows(v, r)[:ROW_BLOCK]
        parts.append(out)
    ln = _layernorm(jnp.concatenate(parts, axis=-1), lg_ref[...], lb_ref[...])
    y_s[r0:r0 + ROW_BLOCK, 0:da] = (ln * jax.nn.sigmoid(ln)).astype(BF16)
    t = j * ts + r0 + lax.broadcasted_iota(jnp.int32, (ROW_BLOCK, 1), 0)
    for gi, w in enumerate(POOL_WINDOWS):
        half = w // 2
        lanes = slice(gi * gp, (gi + 1) * gp)
        xw = p_s[r0 + HALO - SUBLANES:r0 + HALO + ROW_BLOCK + SUBLANES, lanes]
        d = xw
        size = 1
        while size < half:
            d = d + _shift_rows(d, size)
            size *= 2
        s = _shift_rows(d, SUBLANES - half)[:ROW_BLOCK] + d[SUBLANES:SUBLANES + ROW_BLOCK]
        cnt = (jnp.minimum(t + half - 1, seq - 1) - jnp.maximum(t - half, 0) + 1).astype(F32)
        pooled = s / cnt - xw[SUBLANES:SUBLANES + ROW_BLOCK]
        pl_s[r0:r0 + ROW_BLOCK, lanes] = pooled.astype(BF16)


def _even_back(xm_ref, pw_ref, ps_ref, wout_ref, y_s, pl_s):
    da = pl_s.shape[1]
    y_s[:, da:] = (_dot(pl_s[...], pw_ref[...]) * ps_ref[...]).astype(BF16)
    return xm_ref[...] + _dot(y_s[...], wout_ref[...])


def _layer0_kernel(xm_ref, xp_ref, xn_ref, mg_ref, win_ref, cw_ref, cb_ref, lg_ref, lb_ref,
                   pw_ref, ps_ref, wout_ref, fg_ref, wg_ref, wu_ref, wd_ref, o_ref,
                   h_s, u_s, p_s, y_s, pl_s, carry_s, hcarry_s, xprev_s, hprev_s, *, ts, seq, n_tiles):
    g = pl.program_id(0)

    @pl.when(g == 0)
    def _():
        carry_s[...] = jnp.zeros_like(carry_s)
        hcarry_s[...] = jnp.zeros_like(hcarry_s)

    xprev_s[...] = carry_s[...]
    hprev_s[...] = hcarry_s[...]
    j = jnp.minimum(g, n_tiles - 1) % (seq // ts)
    _even_front(j, xm_ref, xp_ref, xn_ref, mg_ref, win_ref, h_s, u_s, p_s, ts=ts, seq=seq)
    row_starts = iter(range(0, ts, ROW_BLOCK))

    def rows():
        r0 = next(row_starts, None)
        if r0 is not None:
            _even_rows(j, r0, cw_ref, cb_ref, lg_ref, lb_ref, u_s, p_s, y_s, pl_s, ts=ts, seq=seq)

    o_ref[...] = _ffn(xprev_s[...], hprev_s[...], wg_ref, wu_ref, wd_ref, between=rows)
    for r0 in row_starts:
        _even_rows(j, r0, cw_ref, cb_ref, lg_ref, lb_ref, u_s, p_s, y_s, pl_s, ts=ts, seq=seq)
    x1 = _even_back(xm_ref, pw_ref, ps_ref, wout_ref, y_s, pl_s)
    carry_s[...] = x1
    hcarry_s[...] = _ffn_in(x1, fg_ref)


def _layer0_call(x2d, seq, mix_g, w_in, conv_w, conv_b, ln_g, ln_b, pool_bd, pool_scale, w_out,
                 ffn_g, wg, wu, wd, ts):
    t, d = x2d.shape
    da = conv_w.shape[1]
    f = wg.shape[1]
    nh = ts // HALO
    n_tiles = t // ts
    last_halo = t // HALO - 1

    def tile(g):
        return jnp.minimum(g, n_tiles - 1)

    kernel = functools.partial(_layer0_kernel, ts=ts, seq=seq, n_tiles=n_tiles)
    return pl.pallas_call(
        kernel,
        grid=(n_tiles + 1,),
        in_specs=[
            pl.BlockSpec((ts, d), lambda g: (tile(g), 0)),
            pl.BlockSpec((HALO, d), lambda g: (jnp.maximum(tile(g) * nh - 1, 0), 0)),
            pl.BlockSpec((HALO, d), lambda g: (jnp.minimum((tile(g) + 1) * nh, last_halo), 0)),
            _const_spec((1, d)), _const_spec(w_in.shape), _const_spec(conv_w.shape),
            _const_spec((1, da)), _const_spec((1, da)), _const_spec((1, da)),
            _const_spec(pool_bd.shape), _const_spec((1, da)), _const_spec(w_out.shape),
            _const_spec((1, d)), _const_spec((d, f)), _const_spec((d, f)), _const_spec((f, d)),
        ],
        out_specs=pl.BlockSpec((ts, d), lambda g: (jnp.maximum(g - 1, 0), 0)),
        out_shape=jax.ShapeDtypeStruct((t, d), F32),
        scratch_shapes=[
            pltpu.VMEM((ts + 2 * HALO, d), BF16),
            pltpu.VMEM((ts + 2 * HALO, da), F32),
            pltpu.VMEM((ts + 2 * HALO, da), F32),
            pltpu.VMEM((ts, d), BF16),
            pltpu.VMEM((ts, da), BF16),
            pltpu.VMEM((ts, d), F32),
            pltpu.VMEM((ts, d), BF16),
            pltpu.VMEM((ts, d), F32),
            pltpu.VMEM((ts, d), BF16),
        ],
        compiler_params=_params(("arbitrary",)),
        name="layer0",
    )(x2d, x2d, x2d, mix_g.reshape(1, d), w_in, conv_w, conv_b.reshape(1, da), ln_g.reshape(1, da),
      ln_b.reshape(1, da), pool_bd, pool_scale.reshape(1, da), w_out, ffn_g.reshape(1, d), wg, wu, wd)


def _odd_in_kernel(x_ref, g_ref, win_ref, cc_ref, sc_ref, vg_ref, vb_ref, sw_ref, sb_ref,
                   ab_ref, yd_ref, ab_s, *, ts):
    dc = cc_ref.shape[0]
    hd = dc // N_HEADS
    h = _rms(x_ref[...], g_ref[...]).astype(BF16)
    z = _dot(h, win_ref[...])
    c = z[:, :dc].astype(BF16)
    for q, m_ref in enumerate((cc_ref, sc_ref)):
        part = _dot(c, m_ref[...])
        for blk in range(dc // LANES):
            lanes = slice(blk * LANES, (blk + 1) * LANES)
            ab_s[blk] = part[:, lanes]
            for parity in range(2):
                ab_ref[q, parity, :, lanes] = ab_s[blk, pl.ds(parity, ts // 2, stride=2), :].astype(BF16)
    zuv = z[:, dc:]
    uv = 0.5 * zuv * (1.0 + lax.erf(zuv * math.sqrt(0.5)))
    u = uv[:, :dc]
    v = uv[:, dc:]
    for hi in range(N_HEADS):
        lanes = slice(hi * hd, (hi + 1) * hd)
        vn = _layernorm(v[:, lanes], vg_ref[:, lanes], vb_ref[:, lanes]).astype(BF16)
        for n in range(ts // CHUNK):
            rows = slice(n * CHUNK, (n + 1) * CHUNK)
            sv = _dot(sw_ref[hi], vn[rows, :]) + sb_ref[:, lanes]
            yd_ref[rows, lanes] = (u[rows, lanes] * sv).astype(BF16)


def _odd_in_call(x, g, w_in, cc_bd, sc_bd, v_ln_g, v_ln_b, spatial_w, sb_full, ts):
    b, s, d = x.shape
    dc = cc_bd.shape[0]
    kernel = functools.partial(_odd_in_kernel, ts=ts)
    return pl.pallas_call(
        kernel,
        grid=(b, s // ts),
        in_specs=[
            pl.BlockSpec((None, ts, d), lambda bi, j: (bi, j, 0)),
            _const_spec((1, d)), _const_spec(w_in.shape), _const_spec(cc_bd.shape),
            _const_spec(sc_bd.shape), _const_spec((1, dc)), _const_spec((1, dc)),
            _const_spec(spatial_w.shape), _const_spec(sb_full.shape),
        ],
        out_specs=[
            pl.BlockSpec((2, 2, ts // 2, dc), lambda bi, j: (0, 0, j, bi)),
            pl.BlockSpec((None, ts, dc), lambda bi, j: (bi, j, 0)),
        ],
        out_shape=[
            jax.ShapeDtypeStruct((2, 2, s // 2, b * dc), BF16),
            jax.ShapeDtypeStruct((b, s, dc), BF16),
        ],
        scratch_shapes=[pltpu.VMEM((dc // LANES, ts, LANES), F32)],
        compiler_params=_params(("parallel", "parallel")),
        name="odd_mixer_in",
    )(x, g.reshape(1, d), w_in, cc_bd, sc_bd, v_ln_g.reshape(1, dc), v_ln_b.reshape(1, dc),
      spatial_w, sb_full)


def _dft_kernel(cm_ref, ab_ref, o_ref, acc_e, acc_o, *, scale):
    k = pl.program_id(2)

    @pl.when(k == 0)
    def _():
        acc_e[...] = jnp.zeros_like(acc_e)
        acc_o[...] = jnp.zeros_like(acc_o)

    acc_e[...] += _dot(cm_ref[0, 0], ab_ref[0, 0]) + _dot(cm_ref[0, 1], ab_ref[1, 0])
    acc_o[...] += _dot(cm_ref[1, 0], ab_ref[0, 1]) + _dot(cm_ref[1, 1], ab_ref[1, 1])

    @pl.when(k == pl.num_programs(2) - 1)
    def _():
        e = acc_e[...]
        o = acc_o[...]
        o_ref[0] = ((e + o) * scale).astype(o_ref.dtype)
        o_ref[1] = ((e - o) * scale).astype(o_ref.dtype)


def _dft_call(cmat, ab, scale, tm, tn, tk):
    _, _, half, nc = ab.shape
    kernel = functools.partial(_dft_kernel, scale=scale)
    out = pl.pallas_call(
        kernel,
        grid=(half // tm, nc // tn, half // tk),
        in_specs=[
            pl.BlockSpec((2, 2, tm, tk), lambda i, n, k: (0, 0, i, k)),
            pl.BlockSpec((2, 2, tk, tn), lambda i, n, k: (0, 0, k, n)),
        ],
        out_specs=pl.BlockSpec((2, tm, tn), lambda i, n, k: (0, i, n)),
        out_shape=jax.ShapeDtypeStruct((2, half, nc), BF16),
        scratch_shapes=[pltpu.VMEM((tm, tn), F32), pltpu.VMEM((tm, tn), F32)],
        compiler_params=_params(("parallel", "parallel", "arbitrary")),
        name="seq_dft",
    )(cmat, ab)
    return out.reshape(2 * half, nc)


def _layer1_out_kernel(x_ref, yp_ref, yd_ref, fw_ref, wout_ref, fg_ref, wg_ref, wu_ref, wd_ref,
                       ng_ref, o_ref):
    yc = _dot(yp_ref[...], fw_ref[...]).astype(BF16)
    y = jnp.concatenate([yc, yd_ref[...]], axis=-1)
    x1 = x_ref[...] + _dot(y, wout_ref[...])
    o_ref[...] = _rms(_ffn(x1, _ffn_in(x1, fg_ref), wg_ref, wu_ref, wd_ref), ng_ref[...])


def _layer1_out_call(x, ypre, yd, fw_bd, w_out, ffn_g, wg, wu, wd, final_g, ts):
    b, s, d = x.shape
    dc = fw_bd.shape[0]
    f = wg.shape[1]
    return pl.pallas_call(
        _layer1_out_kernel,
        grid=(b, s // ts),
        in_specs=[
            pl.BlockSpec((None, ts, d), lambda bi, j: (bi, j, 0)),
            pl.BlockSpec((ts, dc), lambda bi, j: (j, bi)),
            pl.BlockSpec((None, ts, dc), lambda bi, j: (bi, j, 0)),
            _const_spec(fw_bd.shape), _const_spec(w_out.shape),
            _const_spec((1, d)), _const_spec((d, f)), _const_spec((d, f)), _const_spec((f, d)),
            _const_spec((1, d)),
        ],
        out_specs=pl.BlockSpec((None, ts, d), lambda bi, j: (bi, j, 0)),
        out_shape=jax.ShapeDtypeStruct((b, s, d), F32),
        compiler_params=_params(("parallel", "parallel")),
        name="layer1_out",
    )(x, ypre, yd, fw_bd, w_out, ffn_g.reshape(1, d), wg, wu, wd, final_g.reshape(1, d))


def _block_diag(w):
    g, n, _ = w.shape
    eye = jnp.eye(g, dtype=w.dtype)
    return (eye[:, None, :, None] * w[:, :, None, :]).reshape(g * n, g * n)


def _channel_dft_mats(n):
    idx = np.outer(np.arange(n), np.arange(n)) % n
    ang = 2.0 * np.pi * idx / n
    return np.cos(ang).astype(np.float32), np.sin(ang).astype(np.float32)


def _twiddle_kernel(ta_ref, tb_ref, o_ref):
    for blk in range(o_ref.shape[3] // LANES):
        ca = ta_ref[0, :, blk:blk + 1]
        sa = ta_ref[1, :, blk:blk + 1]
        cols = slice(blk * LANES, (blk + 1) * LANES)
        for parity in range(2):
            cb = tb_ref[parity, 0]
            sb = tb_ref[parity, 1]
            o_ref[parity, 0, :, cols] = (ca * cb - sa * sb).astype(o_ref.dtype)
            o_ref[parity, 1, :, cols] = (-(sa * cb) - ca * sb).astype(o_ref.dtype)


def _seq_dft_mats(s, rows):
    half = s // 2
    nblk = half // LANES
    unit = 2.0 * math.pi / s

    def cos_sin(idx):
        ang = (idx % s).astype(F32) * unit
        return jnp.stack([jnp.cos(ang), jnp.sin(ang)])

    j = lax.broadcasted_iota(jnp.int32, (half, nblk), 0)
    blk = lax.broadcasted_iota(jnp.int32, (half, nblk), 1)
    ta = cos_sin(j * (2 * LANES * blk))
    j = lax.broadcasted_iota(jnp.int32, (half, LANES), 0)
    ml = lax.broadcasted_iota(jnp.int32, (half, LANES), 1)
    tb = jnp.stack([cos_sin(j * (2 * ml + parity)) for parity in (0, 1)])
    return pl.pallas_call(
        _twiddle_kernel,
        grid=(half // rows,),
        in_specs=[pl.BlockSpec((2, rows, nblk), lambda i: (0, i, 0)),
                  pl.BlockSpec((2, 2, rows, LANES), lambda i: (0, 0, i, 0))],
        out_specs=pl.BlockSpec((2, 2, rows, half), lambda i: (0, 0, i, 0)),
        out_shape=jax.ShapeDtypeStruct((2, 2, half, half), BF16),
        compiler_params=_params(("parallel",)),
        name="seq_dft_twiddles",
    )(ta, tb)


def kernel(x, mix_norm_g, ffn_norm_g, ev_w_in, ev_conv_w, ev_conv_b, ev_ln_g, ev_ln_b, ev_pool_w,
           ev_pool_scale, ev_w_out, od_w_in, od_fourier_w, od_v_ln_g, od_v_ln_b, od_spatial_w,
           od_spatial_b, od_w_out, ffn_w_gate, ffn_w_up, ffn_w_down, final_norm_g):
    b, s, d = x.shape
    assert mix_norm_g.shape[0] == 2, "one conv/pool layer followed by one fourier/gating layer"
    hc = od_fourier_w.shape[2]
    ts = 512

    cc, sc = _channel_dft_mats(hc)
    cc_bd = _block_diag(jnp.asarray(np.broadcast_to(cc, (N_HEADS, hc, hc)))).astype(BF16)
    sc_bd = _block_diag(jnp.asarray(np.broadcast_to(sc, (N_HEADS, hc, hc)))).astype(BF16)
    cmat = _seq_dft_mats(s, 256)
    dft_scale = 1.0 / math.sqrt(s * hc)
    wg, wu, wd = (w.astype(BF16) for w in (ffn_w_gate, ffn_w_up, ffn_w_down))

    x = _layer0_call(x.reshape(b * s, d), s, mix_norm_g[0], ev_w_in[0].astype(BF16), ev_conv_w[0],
                     ev_conv_b[0], ev_ln_g[0], ev_ln_b[0], _block_diag(ev_pool_w[0]).astype(BF16),
                     ev_pool_scale[0].reshape(-1), ev_w_out[0].astype(BF16),
                     ffn_norm_g[0], wg[0], wu[0], wd[0], ts).reshape(b, s, d)

    sb_full = jnp.broadcast_to(od_spatial_b[0].T[:, :, None], (CHUNK, N_HEADS, hc))
    ab, yd = _odd_in_call(x, mix_norm_g[1], od_w_in[0].astype(BF16), cc_bd, sc_bd,
                          od_v_ln_g[0].reshape(-1), od_v_ln_b[0].reshape(-1),
                          od_spatial_w[0].astype(BF16), sb_full.reshape(CHUNK, N_HEADS * hc), ts)
    ypre = _dft_call(cmat, ab, dft_scale, 1024, 1024, 512)
    return _layer1_out_call(x, ypre, yd, _block_diag(od_fourier_w[0]).astype(BF16),
                            od_w_out[0].astype(BF16), ffn_norm_g[1], wg[1], wu[1], wd[1],
                            final_norm_g, ts)
```

```python
import functools
import math

import numpy as np
import jax
import jax.numpy as jnp
from jax import lax
from jax.experimental import pallas as pl
from jax.experimental.pallas import tpu as pltpu

EPS = 1e-6
CONV_WIDTH = 31
CONV_PAD = CONV_WIDTH // 2
POOL_WINDOWS = (2, 4, 8, 16)
N_HEADS = 4
CHUNK = 128
HALO = 16
ROW_BLOCK = 64
LANE_BLOCK = 128
SUBLANES = 8
LANES = 128
MXU_DIM = 256
VMEM_LIMIT = 56 * 1024 * 1024

F32 = jnp.float32
BF16 = jnp.bfloat16


def _rms(xv, g):
    ms = jnp.mean(xv * xv, axis=-1, keepdims=True)
    return xv * lax.rsqrt(ms + EPS) * g


def _layernorm(xv, g, b):
    mu = jnp.mean(xv, axis=-1, keepdims=True)
    d = xv - mu
    var = jnp.mean(d * d, axis=-1, keepdims=True)
    return d * lax.rsqrt(var + EPS) * g + b


def _dot(a, b):
    return jnp.dot(a, b, preferred_element_type=F32)


def _const_spec(shape, layer=None):
    if layer is None:
        return pl.BlockSpec(shape, lambda *_: (0,) * len(shape), pipeline_mode=pl.Buffered(1))
    return pl.BlockSpec((None,) + tuple(shape[1:]), lambda *_: (layer,) + (0,) * (len(shape) - 1),
                        pipeline_mode=pl.Buffered(1))


def _params(semantics, flags=None):
    return pltpu.CompilerParams(dimension_semantics=semantics, vmem_limit_bytes=VMEM_LIMIT, flags=flags)


def _row_masks():
    sub = lax.broadcasted_iota(jnp.int32, (SUBLANES, LANES), 0)
    return {k: sub < SUBLANES - k for k in range(1, SUBLANES)}


def _shift_rows(a, k, keep):
    if k == 0:
        return a[:-SUBLANES]
    n, w = a.shape
    tiles = pltpu.roll(a.reshape(n // SUBLANES, SUBLANES, w), SUBLANES - k, axis=1)
    return jnp.concatenate([jnp.where(keep[k], tiles[t], tiles[t + 1]) for t in range(n // SUBLANES - 1)],
                           axis=0)


def _ffn_chunks(d_ff):
    step = 4 * MXU_DIM
    return tuple((c, min(c + step, d_ff)) for c in range(0, d_ff, step))


def _ffn_in(x, g_ref):
    return _rms(x, g_ref[...]).astype(BF16)


def _ffn(x, h, wg_ref, wu_ref, wd_ref, between=lambda: None):
    acc = x
    for c0, c1 in _ffn_chunks(wg_ref.shape[1]):
        gate = _dot(h, wg_ref[:, c0:c1])
        between()
        up = _dot(h, wu_ref[:, c0:c1])
        between()
        a = (gate * jax.nn.sigmoid(gate) * up).astype(BF16)
        acc = acc + _dot(a, wd_ref[c0:c1, :])
        between()
    return acc


def _even_front(j, xm_ref, xp_ref, xn_ref, g_ref, win_ref, h_s, u_s, p_s, *, ts, seq):
    da = u_s.shape[1]
    g = g_ref[...]
    h_s[0:HALO, :] = _rms(xp_ref[...], g).astype(BF16)
    h_s[HALO:HALO + ts, :] = _rms(xm_ref[...], g).astype(BF16)
    h_s[HALO + ts:, :] = _rms(xn_ref[...], g).astype(BF16)
    z = _dot(h_s[...], win_ref[...])
    pos = j * ts - HALO + lax.broadcasted_iota(jnp.int32, (ts + 2 * HALO, 1), 0)
    valid = (pos >= 0) & (pos < seq)
    u_s[...] = jnp.where(valid, z[:, :da] * jax.nn.sigmoid(z[:, da:2 * da]), 0.0)
    p_s[0:ts + 2 * HALO, :] = jnp.where(valid, z[:, 2 * da:], 0.0)


def _even_rows(j, r0, keep, cw_ref, cb_ref, lg_ref, lb_ref, u_s, p_s, y_s, pl_s, *, ts, seq):
    da = u_s.shape[1]
    gp = da // N_HEADS
    parts = []
    for l0 in range(0, da, LANE_BLOCK):
        lanes = slice(l0, l0 + LANE_BLOCK)
        out = jnp.broadcast_to(cb_ref[:, lanes], (ROW_BLOCK, LANE_BLOCK))
        for r in range(SUBLANES):
            v = None
            for q in range(pl.cdiv(CONV_WIDTH + HALO - CONV_PAD, SUBLANES)):
                k = SUBLANES * q + r - (HALO - CONV_PAD)
                if 0 <= k < CONV_WIDTH:
                    term = (u_s[r0 + SUBLANES * q:r0 + SUBLANES * q + ROW_BLOCK + SUBLANES, lanes]
                            * cw_ref[k:k + 1, lanes])
                    v = term if v is None else v + term
            out = out + _shift_rows(v, r, keep)
        parts.append(out)
    ln = _layernorm(jnp.concatenate(parts, axis=-1), lg_ref[...], lb_ref[...])
    y_s[r0:r0 + ROW_BLOCK, 0:da] = (ln * jax.nn.sigmoid(ln)).astype(BF16)
    t = j * ts + r0 + lax.broadcasted_iota(jnp.int32, (ROW_BLOCK, 1), 0)
    for gi, w in enumerate(POOL_WINDOWS):
        half = w // 2
        doublings = half.bit_length() - 1
        lanes = slice(gi * gp, (gi + 1) * gp)
        xw = p_s[r0 + HALO - SUBLANES:r0 + HALO + ROW_BLOCK + SUBLANES * doublings, lanes]
        d = xw
        for i in range(doublings):
            d = d[:-SUBLANES] + _shift_rows(d, 1 << i, keep)
        s = _shift_rows(d, SUBLANES - half, keep) + d[SUBLANES:]
        cnt = (jnp.minimum(t + half - 1, seq - 1) - jnp.maximum(t - half, 0) + 1).astype(F32)
        pooled = s / cnt - xw[SUBLANES:SUBLANES + ROW_BLOCK]
        pl_s[r0:r0 + ROW_BLOCK, lanes] = pooled.astype(BF16)


def _even_back(xm_ref, pw_ref, ps_ref, wout_ref, y_s, pl_s):
    da = pl_s.shape[1]
    y_s[:, da:] = (_dot(pl_s[...], pw_ref[...]) * ps_ref[...]).astype(BF16)
    return xm_ref[...] + _dot(y_s[...], wout_ref[...])


def _layer0_kernel(xm_ref, xp_ref, xn_ref, mg_ref, win_ref, cw_ref, cb_ref, lg_ref, lb_ref,
                   pw_ref, ps_ref, wout_ref, fg_ref, wg_ref, wu_ref, wd_ref, o_ref,
                   h_s, u_s, p_s, y_s, pl_s, carry_s, hcarry_s, xprev_s, hprev_s, *, ts, seq, n_tiles):
    g = pl.program_id(0)

    @pl.when(g == 0)
    def _():
        carry_s[...] = jnp.zeros_like(carry_s)
        hcarry_s[...] = jnp.zeros_like(hcarry_s)
        p_s[ts + 2 * HALO:, :] = jnp.zeros_like(p_s[ts + 2 * HALO:, :])

    xprev_s[...] = carry_s[...]
    hprev_s[...] = hcarry_s[...]
    j = jnp.minimum(g, n_tiles - 1) % (seq // ts)
    _even_front(j, xm_ref, xp_ref, xn_ref, mg_ref, win_ref, h_s, u_s, p_s, ts=ts, seq=seq)
    row_starts = iter(range(0, ts, ROW_BLOCK))
    keep = _row_masks()

    def rows():
        r0 = next(row_starts, None)
        if r0 is not None:
            _even_rows(j, r0, keep, cw_ref, cb_ref, lg_ref, lb_ref, u_s, p_s, y_s, pl_s, ts=ts, seq=seq)

    o_ref[...] = _ffn(xprev_s[...], hprev_s[...], wg_ref, wu_ref, wd_ref, between=rows)
    for r0 in row_starts:
        _even_rows(j, r0, keep, cw_ref, cb_ref, lg_ref, lb_ref, u_s, p_s, y_s, pl_s, ts=ts, seq=seq)
    x1 = _even_back(xm_ref, pw_ref, ps_ref, wout_ref, y_s, pl_s)
    carry_s[...] = x1
    hcarry_s[...] = _ffn_in(x1, fg_ref)


def _layer0_call(x2d, seq, mix_g, w_in, conv_w, conv_b, ln_g, ln_b, pool_bd, pool_scale, w_out,
                 ffn_g, wg, wu, wd, ts):
    t, d = x2d.shape
    da = conv_w.shape[1]
    nh = ts // HALO
    n_tiles = t // ts
    last_halo = t // HALO - 1

    def tile(g):
        return jnp.minimum(g, n_tiles - 1)

    kernel = functools.partial(_layer0_kernel, ts=ts, seq=seq, n_tiles=n_tiles)
    return pl.pallas_call(
        kernel,
        grid=(n_tiles + 1,),
        in_specs=[
            pl.BlockSpec((ts, d), lambda g: (tile(g), 0)),
            pl.BlockSpec((HALO, d), lambda g: (jnp.maximum(tile(g) * nh - 1, 0), 0)),
            pl.BlockSpec((HALO, d), lambda g: (jnp.minimum((tile(g) + 1) * nh, last_halo), 0)),
            _const_spec((1, d)), _const_spec(w_in.shape), _const_spec(conv_w.shape),
            _const_spec((1, da)), _const_spec((1, da)), _const_spec((1, da)),
            _const_spec(pool_bd.shape), _const_spec((1, da)), _const_spec(w_out.shape),
            _const_spec((1, d)), _const_spec(wg.shape, 0), _const_spec(wu.shape, 0), _const_spec(wd.shape, 0),
        ],
        out_specs=pl.BlockSpec((ts, d), lambda g: (jnp.maximum(g - 1, 0), 0)),
        out_shape=jax.ShapeDtypeStruct((t, d), F32),
        scratch_shapes=[
            pltpu.VMEM((ts + 2 * HALO, d), BF16),
            pltpu.VMEM((ts + 2 * HALO, da), F32),
            pltpu.VMEM((ts + 2 * HALO + SUBLANES, da), F32),
            pltpu.VMEM((ts, d), BF16),
            pltpu.VMEM((ts, da), BF16),
            pltpu.VMEM((ts, d), F32),
            pltpu.VMEM((ts, d), BF16),
            pltpu.VMEM((ts, d), F32),
            pltpu.VMEM((ts, d), BF16),
        ],
        compiler_params=_params(("arbitrary",)),
        name="layer0",
    )(x2d, x2d, x2d, mix_g.reshape(1, d), w_in, conv_w, conv_b.reshape(1, da), ln_g.reshape(1, da),
      ln_b.reshape(1, da), pool_bd, pool_scale.reshape(1, da), w_out, ffn_g.reshape(1, d), wg, wu, wd)


def _odd_in_kernel(x_ref, g_ref, win_ref, cm_ref, vg_ref, vb_ref, sw_ref, sb_ref,
                   ab_ref, yd_ref, ab_s, *, ts):
    dc = vg_ref.shape[1]
    hd = dc // N_HEADS
    pair = cm_ref.shape[2]
    h = _rms(x_ref[...], g_ref[...]).astype(BF16)
    z = _dot(h, win_ref[...])
    c = z[:, :dc].astype(BF16)
    for q in range(2):
        for pi in range(dc // pair):
            part = _dot(c[:, pi * pair:(pi + 1) * pair], cm_ref[q, pi])
            for sub in range(pair // LANES):
                blk = pi * (pair // LANES) + sub
                lanes = slice(blk * LANES, (blk + 1) * LANES)
                ab_s[blk] = part[:, sub * LANES:(sub + 1) * LANES]
                for parity in range(2):
                    ab_ref[q, parity, :, lanes] = ab_s[blk, pl.ds(parity, ts // 2, stride=2), :].astype(BF16)
    zuv = z[:, dc:]
    uv = 0.5 * zuv * (1.0 + lax.erf(zuv * math.sqrt(0.5)))
    u = uv[:, :dc]
    v = uv[:, dc:]
    for hi in range(N_HEADS):
        lanes = slice(hi * hd, (hi + 1) * hd)
        vn = _layernorm(v[:, lanes], vg_ref[:, lanes], vb_ref[:, lanes]).astype(BF16)
        for n in range(ts // CHUNK):
            rows = slice(n * CHUNK, (n + 1) * CHUNK)
            sv = _dot(sw_ref[hi], vn[rows, :]) + sb_ref[:, lanes]
            yd_ref[rows, lanes] = (u[rows, lanes] * sv).astype(BF16)


def _odd_in_call(x, g, w_in, cmix, v_ln_g, v_ln_b, spatial_w, sb_full, ts):
    b, s, d = x.shape
    dc = v_ln_g.shape[0]
    kernel = functools.partial(_odd_in_kernel, ts=ts)
    return pl.pallas_call(
        kernel,
        grid=(b, s // ts),
        in_specs=[
            pl.BlockSpec((None, ts, d), lambda bi, j: (bi, j, 0)),
            _const_spec((1, d)), _const_spec(w_in.shape), _const_spec(cmix.shape),
            _const_spec((1, dc)), _const_spec((1, dc)),
            _const_spec(spatial_w.shape), _const_spec(sb_full.shape),
        ],
        out_specs=[
            pl.BlockSpec((2, 2, ts // 2, dc), lambda bi, j: (0, 0, j, bi)),
            pl.BlockSpec((None, ts, dc), lambda bi, j: (bi, j, 0)),
        ],
        out_shape=[
            jax.ShapeDtypeStruct((2, 2, s // 2, b * dc), BF16),
            jax.ShapeDtypeStruct((b, s, dc), BF16),
        ],
        scratch_shapes=[pltpu.VMEM((dc // LANES, ts, LANES), F32)],
        compiler_params=_params(("parallel", "parallel")),
        name="odd_mixer_in",
    )(x, g.reshape(1, d), w_in, cmix, v_ln_g.reshape(1, dc), v_ln_b.reshape(1, dc), spatial_w, sb_full)


def _dft_kernel(cm_ref, ab_ref, o_ref, *, scale):
    e = _dot(cm_ref[0, 0], ab_ref[0, 0]) + _dot(cm_ref[0, 1], ab_ref[1, 0])
    o = _dot(cm_ref[1, 0], ab_ref[0, 1]) + _dot(cm_ref[1, 1], ab_ref[1, 1])
    o_ref[0] = ((e + o) * scale).astype(o_ref.dtype)
    o_ref[1] = ((e - o) * scale).astype(o_ref.dtype)


def _dft_call(cmat, ab, scale, tm, tn):
    _, _, half, nc = ab.shape
    kernel = functools.partial(_dft_kernel, scale=scale)
    out = pl.pallas_call(
        kernel,
        grid=(half // tm, nc // tn),
        in_specs=[
            pl.BlockSpec((2, 2, tm, half), lambda i, n: (0, 0, i, 0)),
            pl.BlockSpec((2, 2, half, tn), lambda i, n: (0, 0, 0, n)),
        ],
        out_specs=pl.BlockSpec((2, tm, tn), lambda i, n: (0, i, n)),
        out_shape=jax.ShapeDtypeStruct((2, half, nc), BF16),
        compiler_params=_params(("parallel", "parallel")),
        name="seq_dft",
    )(cmat, ab)
    return out.reshape(2 * half, nc)


def _layer1_out_kernel(x_ref, yc_ref, yd_ref, wout_ref, fg_ref, wg_ref, wu_ref, wd_ref, ng_ref, o_ref):
    y = jnp.concatenate([yc_ref[...], yd_ref[...]], axis=-1)
    x1 = x_ref[...] + _dot(y, wout_ref[...])
    o_ref[...] = _rms(_ffn(x1, _ffn_in(x1, fg_ref), wg_ref, wu_ref, wd_ref), ng_ref[...])


def _layer1_out_call(x, yc, yd, w_out, ffn_g, wg, wu, wd, final_g, ts):
    b, s, d = x.shape
    dc = yd.shape[2]
    return pl.pallas_call(
        _layer1_out_kernel,
        grid=(b, s // ts),
        in_specs=[
            pl.BlockSpec((None, ts, d), lambda bi, j: (bi, j, 0)),
            pl.BlockSpec((ts, dc), lambda bi, j: (j, bi)),
            pl.BlockSpec((None, ts, dc), lambda bi, j: (bi, j, 0)),
            _const_spec(w_out.shape),
            _const_spec((1, d)), _const_spec(wg.shape, 1), _const_spec(wu.shape, 1), _const_spec(wd.shape, 1),
            _const_spec((1, d)),
        ],
        out_specs=pl.BlockSpec((None, ts, d), lambda bi, j: (bi, j, 0)),
        out_shape=jax.ShapeDtypeStruct((b, s, d), F32),
        compiler_params=_params(("parallel", "parallel")),
        name="layer1_out",
    )(x, yc, yd, w_out, ffn_g.reshape(1, d), wg, wu, wd, final_g.reshape(1, d))


def _block_diag(w):
    g, n, _ = w.shape
    eye = jnp.eye(g, dtype=w.dtype)
    return (eye[:, None, :, None] * w[:, :, None, :]).reshape(g * n, g * n)


def _channel_dft_mats(n):
    idx = np.outer(np.arange(n), np.arange(n)) % n
    ang = 2.0 * np.pi * idx / n
    return np.stack([np.cos(ang), np.sin(ang)]).astype(np.float32)


def _fold_kernel(t_ref, w_ref, o_ref):
    n = t_ref.shape[1]
    o_ref[...] = jnp.zeros_like(o_ref)
    for q in range(2):
        for pi in range(o_ref.shape[1]):
            for sub in range(o_ref.shape[2] // n):
                blk = slice(sub * n, (sub + 1) * n)
                head = pi * (o_ref.shape[2] // n) + sub
                o_ref[q, pi, blk, blk] = jnp.dot(t_ref[q], w_ref[head], preferred_element_type=F32,
                                                 precision=lax.Precision.HIGHEST).astype(o_ref.dtype)


def _fold_channel_mix(fourier_w):
    heads, n, _ = fourier_w.shape
    per = MXU_DIM // n
    return pl.pallas_call(
        _fold_kernel,
        out_shape=jax.ShapeDtypeStruct((2, heads // per, per * n, per * n), BF16),
        name="fold_channel_mix",
    )(jnp.asarray(_channel_dft_mats(n)), fourier_w)


def _twiddle_kernel(ta_ref, tb_ref, o_ref):
    for blk in range(o_ref.shape[3] // LANES):
        ca = ta_ref[0, :, blk:blk + 1]
        sa = ta_ref[1, :, blk:blk + 1]
        cols = slice(blk * LANES, (blk + 1) * LANES)
        for parity in range(2):
            cb = tb_ref[parity, 0]
            sb = tb_ref[parity, 1]
            o_ref[parity, 0, :, cols] = (ca * cb - sa * sb).astype(o_ref.dtype)
            o_ref[parity, 1, :, cols] = (-(sa * cb) - ca * sb).astype(o_ref.dtype)


def _seq_dft_mats(s, rows):
    half = s // 2
    nblk = half // LANES
    unit = 2.0 * math.pi / s

    def cos_sin(idx):
        ang = (idx % s).astype(F32) * unit
        return jnp.stack([jnp.cos(ang), jnp.sin(ang)])

    j = lax.broadcasted_iota(jnp.int32, (half, nblk), 0)
    blk = lax.broadcasted_iota(jnp.int32, (half, nblk), 1)
    ta = cos_sin(j * (2 * LANES * blk))
    j = lax.broadcasted_iota(jnp.int32, (half, LANES), 0)
    ml = lax.broadcasted_iota(jnp.int32, (half, LANES), 1)
    tb = jnp.stack([cos_sin(j * (2 * ml + parity)) for parity in (0, 1)])
    return pl.pallas_call(
        _twiddle_kernel,
        grid=(half // rows,),
        in_specs=[pl.BlockSpec((2, rows, nblk), lambda i: (0, i, 0)),
                  pl.BlockSpec((2, 2, rows, LANES), lambda i: (0, 0, i, 0))],
        out_specs=pl.BlockSpec((2, 2, rows, half), lambda i: (0, 0, i, 0)),
        out_shape=jax.ShapeDtypeStruct((2, 2, half, half), BF16),
        compiler_params=_params(("parallel",)),
        name="seq_dft_twiddles",
    )(ta, tb)


def kernel(x, mix_norm_g, ffn_norm_g, ev_w_in, ev_conv_w, ev_conv_b, ev_ln_g, ev_ln_b, ev_pool_w,
           ev_pool_scale, ev_w_out, od_w_in, od_fourier_w, od_v_ln_g, od_v_ln_b, od_spatial_w,
           od_spatial_b, od_w_out, ffn_w_gate, ffn_w_up, ffn_w_down, final_norm_g):
    b, s, d = x.shape
    assert mix_norm_g.shape[0] == 2, "one conv/pool layer followed by one fourier/gating layer"
    hc = od_fourier_w.shape[2]
    ts = 512

    cmat = _seq_dft_mats(s, 256)
    dft_scale = 1.0 / math.sqrt(s * hc)
    wg, wu, wd = (w.astype(BF16) for w in (ffn_w_gate, ffn_w_up, ffn_w_down))

    x = _layer0_call(x.reshape(b * s, d), s, mix_norm_g[0], ev_w_in[0].astype(BF16), ev_conv_w[0],
                     ev_conv_b[0], ev_ln_g[0], ev_ln_b[0], _block_diag(ev_pool_w[0]).astype(BF16),
                     ev_pool_scale[0].reshape(-1), ev_w_out[0].astype(BF16),
                     ffn_norm_g[0], wg, wu, wd, ts).reshape(b, s, d)

    sb_full = jnp.broadcast_to(od_spatial_b[0].T[:, :, None], (CHUNK, N_HEADS, hc))
    ab, yd = _odd_in_call(x, mix_norm_g[1], od_w_in[0].astype(BF16), _fold_channel_mix(od_fourier_w[0]),
                          od_v_ln_g[0].reshape(-1), od_v_ln_b[0].reshape(-1),
                          od_spatial_w[0].astype(BF16), sb_full.reshape(CHUNK, N_HEADS * hc), ts)
    yc = _dft_call(cmat, ab, dft_scale, 512, 512)
    return _layer1_out_call(x, yc, yd, od_w_out[0].astype(BF16), ffn_norm_g[1], wg, wu, wd,
                            final_norm_g, ts)
```

```python
import functools
import math

import numpy as np
import jax
import jax.numpy as jnp
from jax import lax
from jax.experimental import pallas as pl
from jax.experimental.pallas import tpu as pltpu

EPS = 1e-6
CONV_WIDTH = 31
CONV_PAD = CONV_WIDTH // 2
POOL_WINDOWS = (2, 4, 8, 16)
N_HEADS = 4
CHUNK = 128
HALO = 16
ROW_BLOCK = 64
LANE_BLOCK = 128
SUBLANES = 8
LANES = 128
MXU_DIM = 256
VMEM_LIMIT = 56 * 1024 * 1024

F32 = jnp.float32
BF16 = jnp.bfloat16


def _rms(xv, g):
    ms = jnp.mean(xv * xv, axis=-1, keepdims=True)
    return xv * lax.rsqrt(ms + EPS) * g


def _layernorm(xv, g, b):
    mu = jnp.mean(xv, axis=-1, keepdims=True)
    d = xv - mu
    var = jnp.mean(d * d, axis=-1, keepdims=True)
    return d * lax.rsqrt(var + EPS) * g + b


def _dot(a, b):
    return jnp.dot(a, b, preferred_element_type=F32)


def _const_spec(shape, layer=None):
    if layer is None:
        return pl.BlockSpec(shape, lambda *_: (0,) * len(shape), pipeline_mode=pl.Buffered(1))
    return pl.BlockSpec((None,) + tuple(shape[1:]), lambda *_: (layer,) + (0,) * (len(shape) - 1),
                        pipeline_mode=pl.Buffered(1))


def _params(semantics, flags=None):
    return pltpu.CompilerParams(dimension_semantics=semantics, vmem_limit_bytes=VMEM_LIMIT, flags=flags)


def _row_masks():
    sub = lax.broadcasted_iota(jnp.int32, (SUBLANES, LANES), 0)
    return {k: sub < SUBLANES - k for k in range(1, SUBLANES)}


def _shift_rows(a, k, keep):
    if k == 0:
        return a[:-SUBLANES]
    n, w = a.shape
    tiles = pltpu.roll(a.reshape(n // SUBLANES, SUBLANES, w), SUBLANES - k, axis=1)
    return jnp.concatenate([jnp.where(keep[k], tiles[t], tiles[t + 1]) for t in range(n // SUBLANES - 1)],
                           axis=0)


def _ffn_chunks(d_ff):
    step = 4 * MXU_DIM
    return tuple((c, min(c + step, d_ff)) for c in range(0, d_ff, step))


def _ffn_in(x, g_ref):
    return _rms(x, g_ref[...]).astype(BF16)


def _ffn(x, h, wg_ref, wu_ref, wd_ref, between=lambda: None):
    acc = x
    for c0, c1 in _ffn_chunks(wg_ref.shape[1]):
        gate = _dot(h, wg_ref[:, c0:c1])
        between()
        up = _dot(h, wu_ref[:, c0:c1])
        between()
        a = (gate * jax.nn.sigmoid(gate) * up).astype(BF16)
        acc = acc + _dot(a, wd_ref[c0:c1, :])
        between()
    return acc


def _even_front(j, xm_ref, xp_ref, xn_ref, g_ref, win_ref, h_s, u_s, p_s, *, ts, seq):
    da = u_s.shape[1]
    g = g_ref[...]
    h_s[0:HALO, :] = _rms(xp_ref[...], g).astype(BF16)
    h_s[HALO:HALO + ts, :] = _rms(xm_ref[...], g).astype(BF16)
    h_s[HALO + ts:, :] = _rms(xn_ref[...], g).astype(BF16)
    z = _dot(h_s[...], win_ref[...])
    pos = j * ts - HALO + lax.broadcasted_iota(jnp.int32, (ts + 2 * HALO, 1), 0)
    valid = (pos >= 0) & (pos < seq)
    u_s[...] = jnp.where(valid, z[:, :da] * jax.nn.sigmoid(z[:, da:2 * da]), 0.0)
    p_s[0:ts + 2 * HALO, :] = jnp.where(valid, z[:, 2 * da:], 0.0)


def _even_rows(j, r0, keep, cw_ref, cb_ref, lg_ref, lb_ref, u_s, p_s, y_s, pl_s, *, ts, seq):
    da = u_s.shape[1]
    gp = da // N_HEADS
    parts = []
    for l0 in range(0, da, LANE_BLOCK):
        lanes = slice(l0, l0 + LANE_BLOCK)
        out = jnp.broadcast_to(cb_ref[:, lanes], (ROW_BLOCK, LANE_BLOCK))
        for r in range(SUBLANES):
            v = None
            for q in range(pl.cdiv(CONV_WIDTH + HALO - CONV_PAD, SUBLANES)):
                k = SUBLANES * q + r - (HALO - CONV_PAD)
                if 0 <= k < CONV_WIDTH:
                    term = (u_s[r0 + SUBLANES * q:r0 + SUBLANES * q + ROW_BLOCK + SUBLANES, lanes]
                            * cw_ref[k:k + 1, lanes])
                    v = term if v is None else v + term
            out = out + _shift_rows(v, r, keep)
        parts.append(out)
    ln = _layernorm(jnp.concatenate(parts, axis=-1), lg_ref[...], lb_ref[...])
    y_s[r0:r0 + ROW_BLOCK, 0:da] = (ln * jax.nn.sigmoid(ln)).astype(BF16)
    t = j * ts + r0 + lax.broadcasted_iota(jnp.int32, (ROW_BLOCK, 1), 0)
    for gi, w in enumerate(POOL_WINDOWS):
        half = w // 2
        doublings = half.bit_length() - 1
        lanes = slice(gi * gp, (gi + 1) * gp)
        xw = p_s[r0 + HALO - SUBLANES:r0 + HALO + ROW_BLOCK + SUBLANES * doublings, lanes]
        d = xw
        for i in range(doublings):
            d = d[:-SUBLANES] + _shift_rows(d, 1 << i, keep)
        s = _shift_rows(d, SUBLANES - half, keep) + d[SUBLANES:]
        cnt = (jnp.minimum(t + half - 1, seq - 1) - jnp.maximum(t - half, 0) + 1).astype(F32)
        pooled = s / cnt - xw[SUBLANES:SUBLANES + ROW_BLOCK]
        pl_s[r0:r0 + ROW_BLOCK, lanes] = pooled.astype(BF16)


def _even_back(xm_ref, pw_ref, ps_ref, wout_ref, y_s, pl_s):
    da = pl_s.shape[1]
    y_s[:, da:] = (_dot(pl_s[...], pw_ref[...]) * ps_ref[...]).astype(BF16)
    return xm_ref[...] + _dot(y_s[...], wout_ref[...])


def _layer0_kernel(xm_ref, xp_ref, xn_ref, mg_ref, win_ref, cw_ref, cb_ref, lg_ref, lb_ref,
                   pw_ref, ps_ref, wout_ref, fg_ref, wg_ref, wu_ref, wd_ref, o_ref,
                   h_s, u_s, p_s, y_s, pl_s, carry_s, hcarry_s, *, ts, seq, n_tiles):
    g = pl.program_id(0)

    @pl.when(g == 0)
    def _():
        carry_s[...] = jnp.zeros_like(carry_s)
        hcarry_s[...] = jnp.zeros_like(hcarry_s)
        p_s[ts + 2 * HALO:, :] = jnp.zeros_like(p_s[ts + 2 * HALO:, :])

    j = jnp.minimum(g, n_tiles - 1) % (seq // ts)
    _even_front(j, xm_ref, xp_ref, xn_ref, mg_ref, win_ref, h_s, u_s, p_s, ts=ts, seq=seq)
    row_starts = iter(range(0, ts, ROW_BLOCK))
    keep = _row_masks()

    def rows():
        r0 = next(row_starts, None)
        if r0 is not None:
            _even_rows(j, r0, keep, cw_ref, cb_ref, lg_ref, lb_ref, u_s, p_s, y_s, pl_s, ts=ts, seq=seq)

    o_ref[...] = _ffn(carry_s[...], hcarry_s[...], wg_ref, wu_ref, wd_ref, between=rows)
    for r0 in row_starts:
        _even_rows(j, r0, keep, cw_ref, cb_ref, lg_ref, lb_ref, u_s, p_s, y_s, pl_s, ts=ts, seq=seq)
    x1 = _even_back(xm_ref, pw_ref, ps_ref, wout_ref, y_s, pl_s)
    carry_s[...] = x1
    hcarry_s[...] = _ffn_in(x1, fg_ref)


def _layer0_call(x2d, seq, mix_g, w_in, conv_w, conv_b, ln_g, ln_b, pool_bd, pool_scale, w_out,
                 ffn_g, wg, wu, wd, ts):
    t, d = x2d.shape
    da = conv_w.shape[1]
    nh = ts // HALO
    n_tiles = t // ts
    last_halo = t // HALO - 1

    def tile(g):
        return jnp.minimum(g, n_tiles - 1)

    kernel = functools.partial(_layer0_kernel, ts=ts, seq=seq, n_tiles=n_tiles)
    return pl.pallas_call(
        kernel,
        grid=(n_tiles + 1,),
        in_specs=[
            pl.BlockSpec((ts, d), lambda g: (tile(g), 0)),
            pl.BlockSpec((HALO, d), lambda g: (jnp.maximum(tile(g) * nh - 1, 0), 0)),
            pl.BlockSpec((HALO, d), lambda g: (jnp.minimum((tile(g) + 1) * nh, last_halo), 0)),
            _const_spec((1, d)), _const_spec(w_in.shape), _const_spec(conv_w.shape),
            _const_spec((1, da)), _const_spec((1, da)), _const_spec((1, da)),
            _const_spec(pool_bd.shape), _const_spec((1, da)), _const_spec(w_out.shape),
            _const_spec((1, d)), _const_spec(wg.shape, 0), _const_spec(wu.shape, 0), _const_spec(wd.shape, 0),
        ],
        out_specs=pl.BlockSpec((ts, d), lambda g: (jnp.maximum(g - 1, 0), 0)),
        out_shape=jax.ShapeDtypeStruct((t, d), F32),
        scratch_shapes=[
            pltpu.VMEM((ts + 2 * HALO, d), BF16),
            pltpu.VMEM((ts + 2 * HALO, da), F32),
            pltpu.VMEM((ts + 2 * HALO + SUBLANES, da), F32),
            pltpu.VMEM((ts, d), BF16),
            pltpu.VMEM((ts, da), BF16),
            pltpu.VMEM((ts, d), F32),
            pltpu.VMEM((ts, d), BF16),
        ],
        compiler_params=_params(("arbitrary",)),
        name="layer0",
    )(x2d, x2d, x2d, mix_g.reshape(1, d), w_in, conv_w, conv_b.reshape(1, da), ln_g.reshape(1, da),
      ln_b.reshape(1, da), pool_bd, pool_scale.reshape(1, da), w_out, ffn_g.reshape(1, d), wg, wu, wd)


def _odd_in_kernel(x_ref, g_ref, win_ref, cm_ref, vg_ref, vb_ref, sw_ref, sb_ref,
                   abo_ref, abe_ref, yd_ref, ab_s, *, ts):
    dc = vg_ref.shape[1]
    hd = dc // N_HEADS
    pair = cm_ref.shape[2]
    h = _rms(x_ref[...], g_ref[...]).astype(BF16)
    z = _dot(h, win_ref[...])
    c = z[:, :dc].astype(BF16)
    for q in range(2):
        for pi in range(dc // pair):
            part = _dot(c[:, pi * pair:(pi + 1) * pair], cm_ref[q, pi])
            for sub in range(pair // LANES):
                blk = pi * (pair // LANES) + sub
                lanes = slice(blk * LANES, (blk + 1) * LANES)
                ab_s[blk] = part[:, sub * LANES:(sub + 1) * LANES]
                abo_ref[q, :, lanes] = ab_s[blk, pl.ds(1, ts // 2, stride=2), :].astype(BF16)
                for cls in range(2):
                    abe_ref[q, cls, :, lanes] = ab_s[blk, pl.ds(2 * cls, ts // 4, stride=4), :].astype(BF16)
    zuv = z[:, dc:]
    uv = 0.5 * zuv * (1.0 + lax.erf(zuv * math.sqrt(0.5)))
    u = uv[:, :dc]
    v = uv[:, dc:]
    for hi in range(N_HEADS):
        lanes = slice(hi * hd, (hi + 1) * hd)
        vn = _layernorm(v[:, lanes], vg_ref[:, lanes], vb_ref[:, lanes]).astype(BF16)
        for n in range(ts // CHUNK):
            rows = slice(n * CHUNK, (n + 1) * CHUNK)
            sv = _dot(sw_ref[hi], vn[rows, :]) + sb_ref[:, lanes]
            yd_ref[rows, lanes] = (u[rows, lanes] * sv).astype(BF16)


def _odd_in_call(x, g, w_in, cmix, v_ln_g, v_ln_b, spatial_w, sb_full, ts):
    b, s, d = x.shape
    dc = v_ln_g.shape[0]
    kernel = functools.partial(_odd_in_kernel, ts=ts)
    return pl.pallas_call(
        kernel,
        grid=(b, s // ts),
        in_specs=[
            pl.BlockSpec((None, ts, d), lambda bi, j: (bi, j, 0)),
            _const_spec((1, d)), _const_spec(w_in.shape), _const_spec(cmix.shape),
            _const_spec((1, dc)), _const_spec((1, dc)),
            _const_spec(spatial_w.shape), _const_spec(sb_full.shape),
        ],
        out_specs=[
            pl.BlockSpec((2, ts // 2, dc), lambda bi, j: (0, j, bi)),
            pl.BlockSpec((2, 2, ts // 4, dc), lambda bi, j: (0, 0, j, bi)),
            pl.BlockSpec((None, ts, dc), lambda bi, j: (bi, j, 0)),
        ],
        out_shape=[
            jax.ShapeDtypeStruct((2, s // 2, b * dc), BF16),
            jax.ShapeDtypeStruct((2, 2, s // 4, b * dc), BF16),
            jax.ShapeDtypeStruct((b, s, dc), BF16),
        ],
        scratch_shapes=[pltpu.VMEM((dc // LANES, ts, LANES), F32)],
        compiler_params=_params(("parallel", "parallel")),
        name="odd_mixer_in",
    )(x, g.reshape(1, d), w_in, cmix, v_ln_g.reshape(1, dc), v_ln_b.reshape(1, dc), spatial_w, sb_full)


def _dft_kernel(ce_ref, co_lo_ref, co_hi_ref, abe_ref, abo_ref, o_ref, *, scale):
    def cos_minus_sin(m_ref, a, b):
        return _dot(m_ref[0], a) + _dot(m_ref[1], b)

    ee = cos_minus_sin(ce_ref.at[0], abe_ref[0, 0], abe_ref[1, 0])
    eo = cos_minus_sin(ce_ref.at[1], abe_ref[0, 1], abe_ref[1, 1])
    o_lo = cos_minus_sin(co_lo_ref, abo_ref[0], abo_ref[1])
    o_hi = cos_minus_sin(co_hi_ref, abo_ref[0], abo_ref[1])
    e_lo = ee + eo
    e_hi = ee - eo
    for quarter, val in enumerate((e_lo + o_lo, e_hi + o_hi, e_lo - o_lo, e_hi - o_hi)):
        o_ref[quarter] = (val * scale).astype(o_ref.dtype)


def _dft_call(cm_even, cm_odd, ab_even, ab_odd, scale, tm, tn):
    _, half, nc = ab_odd.shape
    quarter = half // 2
    kernel = functools.partial(_dft_kernel, scale=scale)
    out = pl.pallas_call(
        kernel,
        grid=(quarter // tm, nc // tn),
        in_specs=[
            pl.BlockSpec((2, 2, tm, quarter), lambda i, n: (0, 0, i, 0)),
            pl.BlockSpec((None, 2, tm, half), lambda i, n: (0, 0, i, 0)),
            pl.BlockSpec((None, 2, tm, half), lambda i, n: (0, 0, quarter // tm + i, 0)),
            pl.BlockSpec((2, 2, quarter, tn), lambda i, n: (0, 0, 0, n)),
            pl.BlockSpec((2, half, tn), lambda i, n: (0, 0, n)),
        ],
        out_specs=pl.BlockSpec((4, tm, tn), lambda i, n: (0, i, n)),
        out_shape=jax.ShapeDtypeStruct((4, quarter, nc), BF16),
        compiler_params=_params(("parallel", "parallel")),
        name="seq_dft",
    )(cm_even, cm_odd, cm_odd, ab_even, ab_odd)
    return out.reshape(4 * quarter, nc)


def _layer1_out_kernel(x_ref, yc_ref, yd_ref, wout_ref, fg_ref, wg_ref, wu_ref, wd_ref, ng_ref, o_ref):
    y = jnp.concatenate([yc_ref[...], yd_ref[...]], axis=-1)
    x1 = x_ref[...] + _dot(y, wout_ref[...])
    o_ref[...] = _rms(_ffn(x1, _ffn_in(x1, fg_ref), wg_ref, wu_ref, wd_ref), ng_ref[...])


def _layer1_out_call(x, yc, yd, w_out, ffn_g, wg, wu, wd, final_g, ts):
    b, s, d = x.shape
    dc = yd.shape[2]
    return pl.pallas_call(
        _layer1_out_kernel,
        grid=(b, s // ts),
        in_specs=[
            pl.BlockSpec((None, ts, d), lambda bi, j: (bi, j, 0)),
            pl.BlockSpec((ts, dc), lambda bi, j: (j, bi)),
            pl.BlockSpec((None, ts, dc), lambda bi, j: (bi, j, 0)),
            _const_spec(w_out.shape),
            _const_spec((1, d)), _const_spec(wg.shape, 1), _const_spec(wu.shape, 1), _const_spec(wd.shape, 1),
            _const_spec((1, d)),
        ],
        out_specs=pl.BlockSpec((None, ts, d), lambda bi, j: (bi, j, 0)),
        out_shape=jax.ShapeDtypeStruct((b, s, d), F32),
        compiler_params=_params(("parallel", "parallel")),
        name="layer1_out",
    )(x, yc, yd, w_out, ffn_g.reshape(1, d), wg, wu, wd, final_g.reshape(1, d))


def _block_diag(w):
    g, n, _ = w.shape
    eye = jnp.eye(g, dtype=w.dtype)
    return (eye[:, None, :, None] * w[:, :, None, :]).reshape(g * n, g * n)


def _channel_dft_mats(n):
    idx = np.outer(np.arange(n), np.arange(n)) % n
    ang = 2.0 * np.pi * idx / n
    return np.stack([np.cos(ang), np.sin(ang)]).astype(np.float32)


def _fold_kernel(t_ref, w_ref, o_ref):
    n = t_ref.shape[1]
    o_ref[...] = jnp.zeros_like(o_ref)
    for q in range(2):
        for pi in range(o_ref.shape[1]):
            for sub in range(o_ref.shape[2] // n):
                blk = slice(sub * n, (sub + 1) * n)
                head = pi * (o_ref.shape[2] // n) + sub
                o_ref[q, pi, blk, blk] = jnp.dot(t_ref[q], w_ref[head], preferred_element_type=F32,
                                                 precision=lax.Precision.HIGHEST).astype(o_ref.dtype)


def _fold_channel_mix(fourier_w):
    heads, n, _ = fourier_w.shape
    per = MXU_DIM // n
    return pl.pallas_call(
        _fold_kernel,
        out_shape=jax.ShapeDtypeStruct((2, heads // per, per * n, per * n), BF16),
        name="fold_channel_mix",
    )(jnp.asarray(_channel_dft_mats(n)), fourier_w)


def _twiddle_kernel(ta_ref, tb_ref, o_ref):
    for blk in range(o_ref.shape[3] // LANES):
        ca = ta_ref[0, :, blk:blk + 1]
        sa = ta_ref[1, :, blk:blk + 1]
        cols = slice(blk * LANES, (blk + 1) * LANES)
        for pi in range(o_ref.shape[0]):
            cb = tb_ref[pi, 0]
            sb = tb_ref[pi, 1]
            o_ref[pi, 0, :, cols] = (ca * cb - sa * sb).astype(o_ref.dtype)
            o_ref[pi, 1, :, cols] = (-(sa * cb) - ca * sb).astype(o_ref.dtype)


def _seq_dft_mats(s, rows, parities):
    n_par = len(parities)
    half = s // 2
    nblk = half // LANES
    unit = 2.0 * math.pi / s

    def cos_sin(idx):
        ang = (idx % s).astype(F32) * unit
        return jnp.stack([jnp.cos(ang), jnp.sin(ang)])

    j = lax.broadcasted_iota(jnp.int32, (half, nblk), 0)
    blk = lax.broadcasted_iota(jnp.int32, (half, nblk), 1)
    ta = cos_sin(j * (2 * LANES * blk))
    j = lax.broadcasted_iota(jnp.int32, (half, LANES), 0)
    ml = lax.broadcasted_iota(jnp.int32, (half, LANES), 1)
    tb = jnp.stack([cos_sin(j * (2 * ml + parity)) for parity in parities])
    return pl.pallas_call(
        _twiddle_kernel,
        grid=(half // rows,),
        in_specs=[pl.BlockSpec((2, rows, nblk), lambda i: (0, i, 0)),
                  pl.BlockSpec((n_par, 2, rows, LANES), lambda i: (0, 0, i, 0))],
        out_specs=pl.BlockSpec((n_par, 2, rows, half), lambda i: (0, 0, i, 0)),
        out_shape=jax.ShapeDtypeStruct((n_par, 2, half, half), BF16),
        compiler_params=_params(("parallel",)),
        name=f"seq_dft_twiddles_{s}",
    )(ta, tb)


def kernel(x, mix_norm_g, ffn_norm_g, ev_w_in, ev_conv_w, ev_conv_b, ev_ln_g, ev_ln_b, ev_pool_w,
           ev_pool_scale, ev_w_out, od_w_in, od_fourier_w, od_v_ln_g, od_v_ln_b, od_spatial_w,
           od_spatial_b, od_w_out, ffn_w_gate, ffn_w_up, ffn_w_down, final_norm_g):
    b, s, d = x.shape
    assert mix_norm_g.shape[0] == 2, "one conv/pool layer followed by one fourier/gating layer"
    hc = od_fourier_w.shape[2]
    ts = 512

    cm_odd = _seq_dft_mats(s, 256, (1,))
    cm_even = _seq_dft_mats(s // 2, 256, (0, 1))
    dft_scale = 1.0 / math.sqrt(s * hc)
    wg, wu, wd = (w.astype(BF16) for w in (ffn_w_gate, ffn_w_up, ffn_w_down))

    x = _layer0_call(x.reshape(b * s, d), s, mix_norm_g[0], ev_w_in[0].astype(BF16), ev_conv_w[0],
                     ev_conv_b[0], ev_ln_g[0], ev_ln_b[0], _block_diag(ev_pool_w[0]).astype(BF16),
                     ev_pool_scale[0].reshape(-1), ev_w_out[0].astype(BF16),
                     ffn_norm_g[0], wg, wu, wd, ts).reshape(b, s, d)

    sb_full = jnp.broadcast_to(od_spatial_b[0].T[:, :, None], (CHUNK, N_HEADS, hc))
    ab_odd, ab_even, yd = _odd_in_call(x, mix_norm_g[1], od_w_in[0].astype(BF16), _fold_channel_mix(od_fourier_w[0]),
                          od_v_ln_g[0].reshape(-1), od_v_ln_b[0].reshape(-1),
                          od_spatial_w[0].astype(BF16), sb_full.reshape(CHUNK, N_HEADS * hc), ts)
    yc = _dft_call(cm_even, cm_odd, ab_even, ab_odd, dft_scale, 512, 512)
    return _layer1_out_call(x, yc, yd, od_w_out[0].astype(BF16), ffn_norm_g[1], wg, wu, wd,
                            final_norm_g, ts)
```

```python
import functools
import math

import numpy as np
import jax
import jax.numpy as jnp
from jax import lax
from jax.experimental import pallas as pl
from jax.experimental.pallas import tpu as pltpu

EPS = 1e-6
CONV_WIDTH = 31
CONV_PAD = CONV_WIDTH // 2
POOL_WINDOWS = (2, 4, 8, 16)
N_HEADS = 4
CHUNK = 128
HALO = 16
ROW_BLOCK = 64
LANE_BLOCK = 128
SUBLANES = 8
LANES = 128
MXU_DIM = 256
VMEM_LIMIT = 56 * 1024 * 1024

F32 = jnp.float32
BF16 = jnp.bfloat16


def _rms(xv, g):
    ms = jnp.mean(xv * xv, axis=-1, keepdims=True)
    return xv * lax.rsqrt(ms + EPS) * g


def _layernorm(xv, g, b):
    mu = jnp.mean(xv, axis=-1, keepdims=True)
    d = xv - mu
    var = jnp.mean(d * d, axis=-1, keepdims=True)
    return d * lax.rsqrt(var + EPS) * g + b


def _dot(a, b):
    return jnp.dot(a, b, preferred_element_type=F32)


def _const_spec(shape, layer=None):
    if layer is None:
        return pl.BlockSpec(shape, lambda *_: (0,) * len(shape), pipeline_mode=pl.Buffered(1))
    return pl.BlockSpec((None,) + tuple(shape[1:]), lambda *_: (layer,) + (0,) * (len(shape) - 1),
                        pipeline_mode=pl.Buffered(1))


def _params(semantics, flags=None):
    return pltpu.CompilerParams(dimension_semantics=semantics, vmem_limit_bytes=VMEM_LIMIT, flags=flags)


def _row_masks():
    sub = lax.broadcasted_iota(jnp.int32, (SUBLANES, LANES), 0)
    return {k: sub < SUBLANES - k for k in range(1, SUBLANES)}


def _shift_rows(a, k, keep):
    if k == 0:
        return a[:-SUBLANES]
    n, w = a.shape
    tiles = pltpu.roll(a.reshape(n // SUBLANES, SUBLANES, w), SUBLANES - k, axis=1)
    return jnp.concatenate([jnp.where(keep[k], tiles[t], tiles[t + 1]) for t in range(n // SUBLANES - 1)],
                           axis=0)


def _ffn_chunks(d_ff, rows):
    step = max(MXU_DIM, (1 << 19) // rows // MXU_DIM * MXU_DIM)
    return tuple((c, min(c + step, d_ff)) for c in range(0, d_ff, step))


def _ffn_in(x, g_ref):
    return _rms(x, g_ref[...]).astype(BF16)


def _ffn(x, h, wg_ref, wu_ref, wd_ref, between=lambda: None):
    acc = x
    for c0, c1 in _ffn_chunks(wg_ref.shape[1], x.shape[0]):
        gate = _dot(h, wg_ref[:, c0:c1])
        between()
        up = _dot(h, wu_ref[:, c0:c1])
        between()
        a = (gate * jax.nn.sigmoid(gate) * up).astype(BF16)
        acc = acc + _dot(a, wd_ref[c0:c1, :])
        between()
    return acc


def _even_front(j, xm_ref, xp_ref, xn_ref, g_ref, win_ref, h_s, u_s, p_s, *, ts, seq):
    da = u_s.shape[1]
    g = g_ref[...]
    h_s[0:HALO, :] = _rms(xp_ref[...], g).astype(BF16)
    h_s[HALO:HALO + ts, :] = _rms(xm_ref[...], g).astype(BF16)
    h_s[HALO + ts:, :] = _rms(xn_ref[...], g).astype(BF16)
    z = _dot(h_s[...], win_ref[...])
    pos = j * ts - HALO + lax.broadcasted_iota(jnp.int32, (ts + 2 * HALO, 1), 0)
    valid = (pos >= 0) & (pos < seq)
    u_s[...] = jnp.where(valid, z[:, :da] * jax.nn.sigmoid(z[:, da:2 * da]), 0.0)
    p_s[0:ts + 2 * HALO, :] = jnp.where(valid, z[:, 2 * da:], 0.0)


def _even_rows(j, r0, keep, cw_ref, cb_ref, lg_ref, lb_ref, u_s, p_s, y_s, pl_s, *, ts, seq):
    da = u_s.shape[1]
    gp = da // N_HEADS
    parts = []
    for l0 in range(0, da, LANE_BLOCK):
        lanes = slice(l0, l0 + LANE_BLOCK)
        out = jnp.broadcast_to(cb_ref[:, lanes], (ROW_BLOCK, LANE_BLOCK))
        for r in range(SUBLANES):
            v = None
            for q in range(pl.cdiv(CONV_WIDTH + HALO - CONV_PAD, SUBLANES)):
                k = SUBLANES * q + r - (HALO - CONV_PAD)
                if 0 <= k < CONV_WIDTH:
                    term = (u_s[r0 + SUBLANES * q:r0 + SUBLANES * q + ROW_BLOCK + SUBLANES, lanes]
                            * cw_ref[k:k + 1, lanes])
                    v = term if v is None else v + term
            out = out + _shift_rows(v, r, keep)
        parts.append(out)
    ln = _layernorm(jnp.concatenate(parts, axis=-1), lg_ref[...], lb_ref[...])
    y_s[r0:r0 + ROW_BLOCK, 0:da] = (ln * jax.nn.sigmoid(ln)).astype(BF16)
    t = j * ts + r0 + lax.broadcasted_iota(jnp.int32, (ROW_BLOCK, 1), 0)
    for gi, w in enumerate(POOL_WINDOWS):
        half = w // 2
        doublings = half.bit_length() - 1
        lanes = slice(gi * gp, (gi + 1) * gp)
        xw = p_s[r0 + HALO - SUBLANES:r0 + HALO + ROW_BLOCK + SUBLANES * doublings, lanes]
        d = xw
        for i in range(doublings):
            d = d[:-SUBLANES] + _shift_rows(d, 1 << i, keep)
        s = _shift_rows(d, SUBLANES - half, keep) + d[SUBLANES:]
        cnt = (jnp.minimum(t + half - 1, seq - 1) - jnp.maximum(t - half, 0) + 1).astype(F32)
        pooled = s / cnt - xw[SUBLANES:SUBLANES + ROW_BLOCK]
        pl_s[r0:r0 + ROW_BLOCK, lanes] = pooled.astype(BF16)


def _even_back(xm_ref, pw_ref, ps_ref, wout_ref, y_s, pl_s):
    da = pl_s.shape[1]
    y_s[:, da:] = (_dot(pl_s[...], pw_ref[...]) * ps_ref[...]).astype(BF16)
    return xm_ref[...] + _dot(y_s[...], wout_ref[...])


def _layer0_kernel(xm_ref, xp_ref, xn_ref, mg_ref, win_ref, cw_ref, cb_ref, lg_ref, lb_ref,
                   pw_ref, ps_ref, wout_ref, fg_ref, wg_ref, wu_ref, wd_ref, o_ref,
                   h_s, u_s, p_s, y_s, pl_s, carry_s, hcarry_s, *, ts, seq, n_tiles):
    g = pl.program_id(0)

    @pl.when(g == 0)
    def _():
        carry_s[...] = jnp.zeros_like(carry_s)
        hcarry_s[...] = jnp.zeros_like(hcarry_s)
        p_s[ts + 2 * HALO:, :] = jnp.zeros_like(p_s[ts + 2 * HALO:, :])

    j = jnp.minimum(g, n_tiles - 1) % (seq // ts)
    _even_front(j, xm_ref, xp_ref, xn_ref, mg_ref, win_ref, h_s, u_s, p_s, ts=ts, seq=seq)
    row_starts = iter(range(0, ts, ROW_BLOCK))
    keep = _row_masks()

    def rows():
        r0 = next(row_starts, None)
        if r0 is not None:
            _even_rows(j, r0, keep, cw_ref, cb_ref, lg_ref, lb_ref, u_s, p_s, y_s, pl_s, ts=ts, seq=seq)

    o_ref[...] = _ffn(carry_s[...], hcarry_s[...], wg_ref, wu_ref, wd_ref, between=rows)
    for r0 in row_starts:
        _even_rows(j, r0, keep, cw_ref, cb_ref, lg_ref, lb_ref, u_s, p_s, y_s, pl_s, ts=ts, seq=seq)
    x1 = _even_back(xm_ref, pw_ref, ps_ref, wout_ref, y_s, pl_s)
    carry_s[...] = x1
    hcarry_s[...] = _ffn_in(x1, fg_ref)


def _layer0_call(x2d, seq, mix_g, w_in, conv_w, conv_b, ln_g, ln_b, pool_bd, pool_scale, w_out,
                 ffn_g, wg, wu, wd, ts):
    t, d = x2d.shape
    da = conv_w.shape[1]
    nh = ts // HALO
    n_tiles = t // ts
    last_halo = t // HALO - 1

    def tile(g):
        return jnp.minimum(g, n_tiles - 1)

    kernel = functools.partial(_layer0_kernel, ts=ts, seq=seq, n_tiles=n_tiles)
    return pl.pallas_call(
        kernel,
        grid=(n_tiles + 1,),
        in_specs=[
            pl.BlockSpec((ts, d), lambda g: (tile(g), 0)),
            pl.BlockSpec((HALO, d), lambda g: (jnp.maximum(tile(g) * nh - 1, 0), 0)),
            pl.BlockSpec((HALO, d), lambda g: (jnp.minimum((tile(g) + 1) * nh, last_halo), 0)),
            _const_spec((1, d)), _const_spec(w_in.shape), _const_spec(conv_w.shape),
            _const_spec((1, da)), _const_spec((1, da)), _const_spec((1, da)),
            _const_spec(pool_bd.shape), _const_spec((1, da)), _const_spec(w_out.shape),
            _const_spec((1, d)), _const_spec(wg.shape, 0), _const_spec(wu.shape, 0), _const_spec(wd.shape, 0),
        ],
        out_specs=pl.BlockSpec((ts, d), lambda g: (jnp.maximum(g - 1, 0), 0)),
        out_shape=jax.ShapeDtypeStruct((t, d), F32),
        scratch_shapes=[
            pltpu.VMEM((ts + 2 * HALO, d), BF16),
            pltpu.VMEM((ts + 2 * HALO, da), F32),
            pltpu.VMEM((ts + 2 * HALO + SUBLANES, da), F32),
            pltpu.VMEM((ts, d), BF16),
            pltpu.VMEM((ts, da), BF16),
            pltpu.VMEM((ts, d), F32),
            pltpu.VMEM((ts, d), BF16),
        ],
        compiler_params=_params(("arbitrary",)),
        name="layer0",
    )(x2d, x2d, x2d, mix_g.reshape(1, d), w_in, conv_w, conv_b.reshape(1, da), ln_g.reshape(1, da),
      ln_b.reshape(1, da), pool_bd, pool_scale.reshape(1, da), w_out, ffn_g.reshape(1, d), wg, wu, wd)


def _odd_in_kernel(x_ref, g_ref, win_ref, cm_ref, vg_ref, vb_ref, sw_ref, sb_ref,
                   abo_ref, abe_ref, yd_ref, ab_s, *, ts):
    dc = vg_ref.shape[1]
    hd = dc // N_HEADS
    pair = cm_ref.shape[2]
    h = _rms(x_ref[...], g_ref[...]).astype(BF16)
    z = _dot(h, win_ref[...])
    c = z[:, :dc].astype(BF16)
    for q in range(2):
        for pi in range(dc // pair):
            part = _dot(c[:, pi * pair:(pi + 1) * pair], cm_ref[q, pi])
            for sub in range(pair // LANES):
                blk = pi * (pair // LANES) + sub
                lanes = slice(blk * LANES, (blk + 1) * LANES)
                ab_s[blk] = part[:, sub * LANES:(sub + 1) * LANES]
                abo_ref[q, :, lanes] = ab_s[blk, pl.ds(1, ts // 2, stride=2), :].astype(BF16)
                for cls in range(2):
                    abe_ref[q, cls, :, lanes] = ab_s[blk, pl.ds(2 * cls, ts // 4, stride=4), :].astype(BF16)
    zuv = z[:, dc:]
    uv = 0.5 * zuv * (1.0 + lax.erf(zuv * math.sqrt(0.5)))
    u = uv[:, :dc]
    v = uv[:, dc:]
    for hi in range(N_HEADS):
        lanes = slice(hi * hd, (hi + 1) * hd)
        vn = _layernorm(v[:, lanes], vg_ref[:, lanes], vb_ref[:, lanes]).astype(BF16)
        for n in range(ts // CHUNK):
            rows = slice(n * CHUNK, (n + 1) * CHUNK)
            sv = _dot(sw_ref[hi], vn[rows, :]) + sb_ref[:, lanes]
            yd_ref[rows, lanes] = (u[rows, lanes] * sv).astype(BF16)


def _odd_in_call(x, g, w_in, cmix, v_ln_g, v_ln_b, spatial_w, sb_full, ts):
    b, s, d = x.shape
    dc = v_ln_g.shape[0]
    kernel = functools.partial(_odd_in_kernel, ts=ts)
    return pl.pallas_call(
        kernel,
        grid=(b, s // ts),
        in_specs=[
            pl.BlockSpec((None, ts, d), lambda bi, j: (bi, j, 0)),
            _const_spec((1, d)), _const_spec(w_in.shape), _const_spec(cmix.shape),
            _const_spec((1, dc)), _const_spec((1, dc)),
            _const_spec(spatial_w.shape), _const_spec(sb_full.shape),
        ],
        out_specs=[
            pl.BlockSpec((2, ts // 2, dc), lambda bi, j: (0, j, bi)),
            pl.BlockSpec((2, 2, ts // 4, dc), lambda bi, j: (0, 0, j, bi)),
            pl.BlockSpec((None, ts, dc), lambda bi, j: (bi, j, 0)),
        ],
        out_shape=[
            jax.ShapeDtypeStruct((2, s // 2, b * dc), BF16),
            jax.ShapeDtypeStruct((2, 2, s // 4, b * dc), BF16),
            jax.ShapeDtypeStruct((b, s, dc), BF16),
        ],
        scratch_shapes=[pltpu.VMEM((dc // LANES, ts, LANES), F32)],
        compiler_params=_params(("parallel", "parallel")),
        name="odd_mixer_in",
    )(x, g.reshape(1, d), w_in, cmix, v_ln_g.reshape(1, dc), v_ln_b.reshape(1, dc), spatial_w, sb_full)


def _dft_kernel(ce_ref, co_lo_ref, co_hi_ref, abe_ref, abo_ref, o_ref, *, scale):
    def cos_minus_sin(m_ref, a, b):
        return _dot(m_ref[0], a) + _dot(m_ref[1], b)

    ee = cos_minus_sin(ce_ref.at[0], abe_ref[0, 0], abe_ref[1, 0])
    eo = cos_minus_sin(ce_ref.at[1], abe_ref[0, 1], abe_ref[1, 1])
    o_lo = cos_minus_sin(co_lo_ref, abo_ref[0], abo_ref[1])
    o_hi = cos_minus_sin(co_hi_ref, abo_ref[0], abo_ref[1])
    e_lo = ee + eo
    e_hi = ee - eo
    for quarter, val in enumerate((e_lo + o_lo, e_hi + o_hi, e_lo - o_lo, e_hi - o_hi)):
        o_ref[quarter] = (val * scale).astype(o_ref.dtype)


def _dft_call(cm_even, cm_odd, ab_even, ab_odd, scale, tm, tn):
    _, half, nc = ab_odd.shape
    quarter = half // 2
    kernel = functools.partial(_dft_kernel, scale=scale)
    out = pl.pallas_call(
        kernel,
        grid=(quarter // tm, nc // tn),
        in_specs=[
            pl.BlockSpec((2, 2, tm, quarter), lambda i, n: (0, 0, i, 0)),
            pl.BlockSpec((None, 2, tm, half), lambda i, n: (0, 0, i, 0)),
            pl.BlockSpec((None, 2, tm, half), lambda i, n: (0, 0, quarter // tm + i, 0)),
            pl.BlockSpec((2, 2, quarter, tn), lambda i, n: (0, 0, 0, n)),
            pl.BlockSpec((2, half, tn), lambda i, n: (0, 0, n)),
        ],
        out_specs=pl.BlockSpec((4, tm, tn), lambda i, n: (0, i, n)),
        out_shape=jax.ShapeDtypeStruct((4, quarter, nc), BF16),
        compiler_params=_params(("parallel", "parallel")),
        name="seq_dft",
    )(cm_even, cm_odd, cm_odd, ab_even, ab_odd)
    return out.reshape(4 * quarter, nc)


def _layer1_out_kernel(x_ref, yc_ref, yd_ref, wout_ref, fg_ref, wg_ref, wu_ref, wd_ref, ng_ref, o_ref):
    y = jnp.concatenate([yc_ref[...], yd_ref[...]], axis=-1)
    x1 = x_ref[...] + _dot(y, wout_ref[...])
    o_ref[...] = _rms(_ffn(x1, _ffn_in(x1, fg_ref), wg_ref, wu_ref, wd_ref), ng_ref[...])


def _layer1_out_call(x, yc, yd, w_out, ffn_g, wg, wu, wd, final_g, ts):
    b, s, d = x.shape
    dc = yd.shape[2]
    return pl.pallas_call(
        _layer1_out_kernel,
        grid=(b, s // ts),
        in_specs=[
            pl.BlockSpec((None, ts, d), lambda bi, j: (bi, j, 0)),
            pl.BlockSpec((ts, dc), lambda bi, j: (j, bi)),
            pl.BlockSpec((None, ts, dc), lambda bi, j: (bi, j, 0)),
            _const_spec(w_out.shape),
            _const_spec((1, d)), _const_spec(wg.shape, 1), _const_spec(wu.shape, 1), _const_spec(wd.shape, 1),
            _const_spec((1, d)),
        ],
        out_specs=pl.BlockSpec((None, ts, d), lambda bi, j: (bi, j, 0)),
        out_shape=jax.ShapeDtypeStruct((b, s, d), F32),
        compiler_params=_params(("parallel", "parallel")),
        name="layer1_out",
    )(x, yc, yd, w_out, ffn_g.reshape(1, d), wg, wu, wd, final_g.reshape(1, d))


def _block_diag(w):
    g, n, _ = w.shape
    eye = jnp.eye(g, dtype=w.dtype)
    return (eye[:, None, :, None] * w[:, :, None, :]).reshape(g * n, g * n)


def _channel_dft_mats(n):
    idx = np.outer(np.arange(n), np.arange(n)) % n
    ang = 2.0 * np.pi * idx / n
    return np.stack([np.cos(ang), np.sin(ang)]).astype(np.float32)


def _fold_kernel(t_ref, w_ref, o_ref):
    n = t_ref.shape[1]
    o_ref[...] = jnp.zeros_like(o_ref)
    for q in range(2):
        for pi in range(o_ref.shape[1]):
            for sub in range(o_ref.shape[2] // n):
                blk = slice(sub * n, (sub + 1) * n)
                head = pi * (o_ref.shape[2] // n) + sub
                o_ref[q, pi, blk, blk] = jnp.dot(t_ref[q], w_ref[head], preferred_element_type=F32,
                                                 precision=lax.Precision.HIGHEST).astype(o_ref.dtype)


def _fold_channel_mix(fourier_w):
    heads, n, _ = fourier_w.shape
    per = MXU_DIM // n
    return pl.pallas_call(
        _fold_kernel,
        out_shape=jax.ShapeDtypeStruct((2, heads // per, per * n, per * n), BF16),
        name="fold_channel_mix",
    )(jnp.asarray(_channel_dft_mats(n)), fourier_w)


def _twiddle_kernel(ta_ref, tb_ref, o_ref):
    for blk in range(o_ref.shape[3] // LANES):
        ca = ta_ref[0, :, blk:blk + 1]
        sa = ta_ref[1, :, blk:blk + 1]
        cols = slice(blk * LANES, (blk + 1) * LANES)
        for pi in range(o_ref.shape[0]):
            cb = tb_ref[pi, 0]
            sb = tb_ref[pi, 1]
            o_ref[pi, 0, :, cols] = (ca * cb - sa * sb).astype(o_ref.dtype)
            o_ref[pi, 1, :, cols] = (-(sa * cb) - ca * sb).astype(o_ref.dtype)


def _seq_dft_mats(s, rows, parities):
    n_par = len(parities)
    half = s // 2
    nblk = half // LANES
    unit = 2.0 * math.pi / s

    def cos_sin(idx):
        ang = (idx % s).astype(F32) * unit
        return jnp.stack([jnp.cos(ang), jnp.sin(ang)])

    j = lax.broadcasted_iota(jnp.int32, (half, nblk), 0)
    blk = lax.broadcasted_iota(jnp.int32, (half, nblk), 1)
    ta = cos_sin(j * (2 * LANES * blk))
    j = lax.broadcasted_iota(jnp.int32, (half, LANES), 0)
    ml = lax.broadcasted_iota(jnp.int32, (half, LANES), 1)
    tb = jnp.stack([cos_sin(j * (2 * ml + parity)) for parity in parities])
    return pl.pallas_call(
        _twiddle_kernel,
        grid=(half // rows,),
        in_specs=[pl.BlockSpec((2, rows, nblk), lambda i: (0, i, 0)),
                  pl.BlockSpec((n_par, 2, rows, LANES), lambda i: (0, 0, i, 0))],
        out_specs=pl.BlockSpec((n_par, 2, rows, half), lambda i: (0, 0, i, 0)),
        out_shape=jax.ShapeDtypeStruct((n_par, 2, half, half), BF16),
        compiler_params=_params(("parallel",)),
        name=f"seq_dft_twiddles_{s}",
    )(ta, tb)


def kernel(x, mix_norm_g, ffn_norm_g, ev_w_in, ev_conv_w, ev_conv_b, ev_ln_g, ev_ln_b, ev_pool_w,
           ev_pool_scale, ev_w_out, od_w_in, od_fourier_w, od_v_ln_g, od_v_ln_b, od_spatial_w,
           od_spatial_b, od_w_out, ffn_w_gate, ffn_w_up, ffn_w_down, final_norm_g):
    b, s, d = x.shape
    assert mix_norm_g.shape[0] == 2, "one conv/pool layer followed by one fourier/gating layer"
    hc = od_fourier_w.shape[2]
    ts = 512
    ts1 = 1024

    cm_odd = _seq_dft_mats(s, 256, (1,))
    cm_even = _seq_dft_mats(s // 2, 256, (0, 1))
    dft_scale = 1.0 / math.sqrt(s * hc)
    wg, wu, wd = (w.astype(BF16) for w in (ffn_w_gate, ffn_w_up, ffn_w_down))

    x = _layer0_call(x.reshape(b * s, d), s, mix_norm_g[0], ev_w_in[0].astype(BF16), ev_conv_w[0],
                     ev_conv_b[0], ev_ln_g[0], ev_ln_b[0], _block_diag(ev_pool_w[0]).astype(BF16),
                     ev_pool_scale[0].reshape(-1), ev_w_out[0].astype(BF16),
                     ffn_norm_g[0], wg, wu, wd, ts).reshape(b, s, d)

    sb_full = jnp.broadcast_to(od_spatial_b[0].T[:, :, None], (CHUNK, N_HEADS, hc))
    ab_odd, ab_even, yd = _odd_in_call(
        x, mix_norm_g[1], od_w_in[0].astype(BF16), _fold_channel_mix(od_fourier_w[0]),
        od_v_ln_g[0].reshape(-1), od_v_ln_b[0].reshape(-1), od_spatial_w[0].astype(BF16),
        sb_full.reshape(CHUNK, N_HEADS * hc), ts1)
    yc = _dft_call(cm_even, cm_odd, ab_even, ab_odd, dft_scale, 512, 512)
    return _layer1_out_call(x, yc, yd, od_w_out[0].astype(BF16), ffn_norm_g[1], wg, wu, wd,
                            final_norm_g, ts1)
```

```python
import functools
import math

import numpy as np
import jax
import jax.numpy as jnp
from jax import lax
from jax.experimental import pallas as pl
from jax.experimental.pallas import tpu as pltpu

EPS = 1e-6
CONV_WIDTH = 31
CONV_PAD = CONV_WIDTH // 2
POOL_WINDOWS = (2, 4, 8, 16)
N_HEADS = 4
CHUNK = 128
HALO = 16
ROW_BLOCK = 64
LANE_BLOCK = 128
SUBLANES = 8
LANES = 128
MXU_DIM = 256
VMEM_LIMIT = 56 * 1024 * 1024

F32 = jnp.float32
BF16 = jnp.bfloat16


def _rms(xv, g):
    ms = jnp.mean(xv * xv, axis=-1, keepdims=True)
    return xv * lax.rsqrt(ms + EPS) * g


def _layernorm(xv, g, b):
    mu = jnp.mean(xv, axis=-1, keepdims=True)
    d = xv - mu
    var = jnp.mean(d * d, axis=-1, keepdims=True)
    return d * lax.rsqrt(var + EPS) * g + b


def _dot(a, b):
    return jnp.dot(a, b, preferred_element_type=F32)


def _const_spec(shape, layer=None):
    if layer is None:
        return pl.BlockSpec(shape, lambda *_: (0,) * len(shape), pipeline_mode=pl.Buffered(1))
    return pl.BlockSpec((None,) + tuple(shape[1:]), lambda *_: (layer,) + (0,) * (len(shape) - 1),
                        pipeline_mode=pl.Buffered(1))


def _params(semantics, flags=None):
    return pltpu.CompilerParams(dimension_semantics=semantics, vmem_limit_bytes=VMEM_LIMIT, flags=flags)


def _row_masks():
    sub = lax.broadcasted_iota(jnp.int32, (SUBLANES, LANES), 0)
    return {k: sub < SUBLANES - k for k in range(1, SUBLANES)}


def _shift_rows(a, k, keep):
    if k == 0:
        return a[:-SUBLANES]
    n, w = a.shape
    tiles = pltpu.roll(a.reshape(n // SUBLANES, SUBLANES, w), SUBLANES - k, axis=1)
    return jnp.concatenate([jnp.where(keep[k], tiles[t], tiles[t + 1]) for t in range(n // SUBLANES - 1)],
                           axis=0)


def _ffn_chunks(d_ff, rows):
    step = max(MXU_DIM, (1 << 19) // rows // MXU_DIM * MXU_DIM)
    return tuple((c, min(c + step, d_ff)) for c in range(0, d_ff, step))


def _ffn_in(x, g_ref):
    return _rms(x, g_ref[...]).astype(BF16)


def _ffn(x, h, wg_ref, wu_ref, wd_ref):
    acc = x
    for c0, c1 in _ffn_chunks(wg_ref.shape[1], x.shape[0]):
        gate = _dot(h, wg_ref[:, c0:c1])
        up = _dot(h, wu_ref[:, c0:c1])
        a = (gate * jax.nn.sigmoid(gate) * up).astype(BF16)
        acc = acc + _dot(a, wd_ref[c0:c1, :])
    return acc


def _even_mixer_branches(j, xm_ref, xp_ref, xn_ref, g_ref, win_ref, cw_ref, cb_ref, lg_ref, lb_ref,
                         h_s, u_s, p_s, c_s, y_s, pl_s, *, ts, seq):
    da = u_s.shape[1]
    gp = da // N_HEADS
    rows_in = ts + 2 * HALO
    keep = _row_masks()
    g = g_ref[...]
    h_s[0:HALO, :] = _rms(xp_ref[...], g).astype(BF16)
    h_s[HALO:HALO + ts, :] = _rms(xm_ref[...], g).astype(BF16)
    h_s[HALO + ts:, :] = _rms(xn_ref[...], g).astype(BF16)
    pos = j * ts - HALO + lax.broadcasted_iota(jnp.int32, (rows_in, 1), 0)
    valid = (pos >= 0) & (pos < seq)

    for l0 in range(0, da, LANE_BLOCK):
        lanes = slice(l0, l0 + LANE_BLOCK)
        z = _dot(h_s[...], win_ref[:, 2 * l0:2 * (l0 + LANE_BLOCK)])
        u_s[:, lanes] = jnp.where(valid, z[:, :LANE_BLOCK] * jax.nn.sigmoid(z[:, LANE_BLOCK:]), 0.0)
        for r0 in range(0, ts, ROW_BLOCK):
            out = jnp.broadcast_to(cb_ref[:, lanes], (ROW_BLOCK, LANE_BLOCK))
            for r in range(SUBLANES):
                v = None
                for q in range(pl.cdiv(CONV_WIDTH + HALO - CONV_PAD, SUBLANES)):
                    k = SUBLANES * q + r - (HALO - CONV_PAD)
                    if 0 <= k < CONV_WIDTH:
                        term = (u_s[r0 + SUBLANES * q:r0 + SUBLANES * q + ROW_BLOCK + SUBLANES, lanes]
                                * cw_ref[k:k + 1, lanes])
                        v = term if v is None else v + term
                out = out + _shift_rows(v, r, keep)
            c_s[r0:r0 + ROW_BLOCK, lanes] = out

    p_s[0:rows_in, :] = jnp.where(valid, _dot(h_s[...], win_ref[:, 2 * da:]), 0.0)
    for r0 in range(0, ts, ROW_BLOCK):
        ln = _layernorm(c_s[r0:r0 + ROW_BLOCK, :], lg_ref[...], lb_ref[...])
        y_s[r0:r0 + ROW_BLOCK, 0:da] = (ln * jax.nn.sigmoid(ln)).astype(BF16)
        t = j * ts + r0 + lax.broadcasted_iota(jnp.int32, (ROW_BLOCK, 1), 0)
        for gi, w in enumerate(POOL_WINDOWS):
            half = w // 2
            doublings = half.bit_length() - 1
            lanes = slice(gi * gp, (gi + 1) * gp)
            xw = p_s[r0 + HALO - SUBLANES:r0 + HALO + ROW_BLOCK + SUBLANES * doublings, lanes]
            d = xw
            for i in range(doublings):
                d = d[:-SUBLANES] + _shift_rows(d, 1 << i, keep)
            s = _shift_rows(d, SUBLANES - half, keep) + d[SUBLANES:]
            cnt = (jnp.minimum(t + half - 1, seq - 1) - jnp.maximum(t - half, 0) + 1).astype(F32)
            pooled = s / cnt - xw[SUBLANES:SUBLANES + ROW_BLOCK]
            pl_s[r0:r0 + ROW_BLOCK, lanes] = pooled.astype(BF16)


def _even_back(xm_ref, pw_ref, ps_ref, wout_ref, y_s, pl_s):
    da = pl_s.shape[1]
    y_s[:, da:] = (_dot(pl_s[...], pw_ref[...]) * ps_ref[...]).astype(BF16)
    return xm_ref[...] + _dot(y_s[...], wout_ref[...])


def _layer0_kernel(xm_ref, xp_ref, xn_ref, mg_ref, win_ref, cw_ref, cb_ref, lg_ref, lb_ref,
                   pw_ref, ps_ref, wout_ref, fg_ref, wg_ref, wu_ref, wd_ref, o_ref,
                   h_s, u_s, p_s, c_s, y_s, pl_s, carry_s, hcarry_s, *, ts, seq, n_tiles):
    g = pl.program_id(0)

    @pl.when(g == 0)
    def _():
        carry_s[...] = jnp.zeros_like(carry_s)
        hcarry_s[...] = jnp.zeros_like(hcarry_s)
        p_s[ts + 2 * HALO:, :] = jnp.zeros_like(p_s[ts + 2 * HALO:, :])

    j = jnp.minimum(g, n_tiles - 1) % (seq // ts)
    _even_mixer_branches(j, xm_ref, xp_ref, xn_ref, mg_ref, win_ref, cw_ref, cb_ref, lg_ref, lb_ref,
                         h_s, u_s, p_s, c_s, y_s, pl_s, ts=ts, seq=seq)
    o_ref[...] = _ffn(carry_s[...], hcarry_s[...], wg_ref, wu_ref, wd_ref)
    x1 = _even_back(xm_ref, pw_ref, ps_ref, wout_ref, y_s, pl_s)
    carry_s[...] = x1
    hcarry_s[...] = _ffn_in(x1, fg_ref)


def _layer0_call(x2d, seq, mix_g, w_in, conv_w, conv_b, ln_g, ln_b, pool_bd, pool_scale, w_out,
                 ffn_g, wg, wu, wd, ts):
    t, d = x2d.shape
    da = conv_w.shape[1]
    nh = ts // HALO
    n_tiles = t // ts
    last_halo = t // HALO - 1
    glu = w_in[:, :2 * da].reshape(d, 2, da // LANE_BLOCK, LANE_BLOCK).transpose(0, 2, 1, 3)
    w_in = jnp.concatenate([glu.reshape(d, 2 * da), w_in[:, 2 * da:]], axis=1)

    def tile(g):
        return jnp.minimum(g, n_tiles - 1)

    kernel = functools.partial(_layer0_kernel, ts=ts, seq=seq, n_tiles=n_tiles)
    return pl.pallas_call(
        kernel,
        grid=(n_tiles + 1,),
        in_specs=[
            pl.BlockSpec((ts, d), lambda g: (tile(g), 0)),
            pl.BlockSpec((HALO, d), lambda g: (jnp.maximum(tile(g) * nh - 1, 0), 0)),
            pl.BlockSpec((HALO, d), lambda g: (jnp.minimum((tile(g) + 1) * nh, last_halo), 0)),
            _const_spec((1, d)), _const_spec(w_in.shape), _const_spec(conv_w.shape),
            _const_spec((1, da)), _const_spec((1, da)), _const_spec((1, da)),
            _const_spec(pool_bd.shape), _const_spec((1, da)), _const_spec(w_out.shape),
            _const_spec((1, d)), _const_spec(wg.shape, 0), _const_spec(wu.shape, 0), _const_spec(wd.shape, 0),
        ],
        out_specs=pl.BlockSpec((ts, d), lambda g: (jnp.maximum(g - 1, 0), 0)),
        out_shape=jax.ShapeDtypeStruct((t, d), F32),
        scratch_shapes=[
            pltpu.VMEM((ts + 2 * HALO, d), BF16),
            pltpu.VMEM((ts + 2 * HALO, da), F32),
            pltpu.VMEM((ts + 2 * HALO + SUBLANES, da), F32),
            pltpu.VMEM((ts, da), F32),
            pltpu.VMEM((ts, d), BF16),
            pltpu.VMEM((ts, da), BF16),
            pltpu.VMEM((ts, d), F32),
            pltpu.VMEM((ts, d), BF16),
        ],
        compiler_params=_params(("arbitrary",)),
        name="layer0",
    )(x2d, x2d, x2d, mix_g.reshape(1, d), w_in, conv_w, conv_b.reshape(1, da), ln_g.reshape(1, da),
      ln_b.reshape(1, da), pool_bd, pool_scale.reshape(1, da), w_out, ffn_g.reshape(1, d), wg, wu, wd)


def _odd_in_kernel(x_ref, g_ref, win_ref, cm_ref, vg_ref, vb_ref, sw_ref, sb_ref,
                   abo_ref, abe_ref, yd_ref, ab_s, *, ts):
    dc = vg_ref.shape[1]
    hd = dc // N_HEADS
    pair = cm_ref.shape[2]
    h = _rms(x_ref[...], g_ref[...]).astype(BF16)
    z = _dot(h, win_ref[...])
    c = z[:, :dc].astype(BF16)
    for q in range(2):
        for pi in range(dc // pair):
            part = _dot(c[:, pi * pair:(pi + 1) * pair], cm_ref[q, pi])
            for sub in range(pair // LANES):
                blk = pi * (pair // LANES) + sub
                lanes = slice(blk * LANES, (blk + 1) * LANES)
                ab_s[blk] = part[:, sub * LANES:(sub + 1) * LANES]
                abo_ref[q, :, lanes] = ab_s[blk, pl.ds(1, ts // 2, stride=2), :].astype(BF16)
                for cls in range(2):
                    abe_ref[q, cls, :, lanes] = ab_s[blk, pl.ds(2 * cls, ts // 4, stride=4), :].astype(BF16)
    zuv = z[:, dc:]
    uv = 0.5 * zuv * (1.0 + lax.erf(zuv * math.sqrt(0.5)))
    u = uv[:, :dc]
    v = uv[:, dc:]
    for hi in range(N_HEADS):
        lanes = slice(hi * hd, (hi + 1) * hd)
        vn = _layernorm(v[:, lanes], vg_ref[:, lanes], vb_ref[:, lanes]).astype(BF16)
        for n in range(ts // CHUNK):
            rows = slice(n * CHUNK, (n + 1) * CHUNK)
            sv = _dot(sw_ref[hi], vn[rows, :]) + sb_ref[:, lanes]
            yd_ref[rows, lanes] = (u[rows, lanes] * sv).astype(BF16)


def _odd_in_call(x, g, w_in, cmix, v_ln_g, v_ln_b, spatial_w, sb_full, ts):
    b, s, d = x.shape
    dc = v_ln_g.shape[0]
    kernel = functools.partial(_odd_in_kernel, ts=ts)
    return pl.pallas_call(
        kernel,
        grid=(b, s // ts),
        in_specs=[
            pl.BlockSpec((None, ts, d), lambda bi, j: (bi, j, 0)),
            _const_spec((1, d)), _const_spec(w_in.shape), _const_spec(cmix.shape),
            _const_spec((1, dc)), _const_spec((1, dc)),
            _const_spec(spatial_w.shape), _const_spec(sb_full.shape),
        ],
        out_specs=[
            pl.BlockSpec((2, ts // 2, dc), lambda bi, j: (0, j, bi)),
            pl.BlockSpec((2, 2, ts // 4, dc), lambda bi, j: (0, 0, j, bi)),
            pl.BlockSpec((None, ts, dc), lambda bi, j: (bi, j, 0)),
        ],
        out_shape=[
            jax.ShapeDtypeStruct((2, s // 2, b * dc), BF16),
            jax.ShapeDtypeStruct((2, 2, s // 4, b * dc), BF16),
            jax.ShapeDtypeStruct((b, s, dc), BF16),
        ],
        scratch_shapes=[pltpu.VMEM((dc // LANES, ts, LANES), F32)],
        compiler_params=_params(("parallel", "parallel")),
        name="odd_mixer_in",
    )(x, g.reshape(1, d), w_in, cmix, v_ln_g.reshape(1, dc), v_ln_b.reshape(1, dc), spatial_w, sb_full)


def _dft_kernel(ce_ref, co_lo_ref, co_hi_ref, abe_ref, abo_ref, o_ref, *, scale):
    def cos_minus_sin(m_ref, a, b):
        return _dot(m_ref[0], a) + _dot(m_ref[1], b)

    ee = cos_minus_sin(ce_ref.at[0], abe_ref[0, 0], abe_ref[1, 0])
    eo = cos_minus_sin(ce_ref.at[1], abe_ref[0, 1], abe_ref[1, 1])
    o_lo = cos_minus_sin(co_lo_ref, abo_ref[0], abo_ref[1])
    o_hi = cos_minus_sin(co_hi_ref, abo_ref[0], abo_ref[1])
    e_lo = ee + eo
    e_hi = ee - eo
    for quarter, val in enumerate((e_lo + o_lo, e_hi + o_hi, e_lo - o_lo, e_hi - o_hi)):
        o_ref[quarter] = (val * scale).astype(o_ref.dtype)


def _dft_call(cm_even, cm_odd, ab_even, ab_odd, scale, tm, tn):
    _, half, nc = ab_odd.shape
    quarter = half // 2
    kernel = functools.partial(_dft_kernel, scale=scale)
    out = pl.pallas_call(
        kernel,
        grid=(quarter // tm, nc // tn),
        in_specs=[
            pl.BlockSpec((2, 2, tm, quarter), lambda i, n: (0, 0, i, 0)),
            pl.BlockSpec((None, 2, tm, half), lambda i, n: (0, 0, i, 0)),
            pl.BlockSpec((None, 2, tm, half), lambda i, n: (0, 0, quarter // tm + i, 0)),
            pl.BlockSpec((2, 2, quarter, tn), lambda i, n: (0, 0, 0, n)),
            pl.BlockSpec((2, half, tn), lambda i, n: (0, 0, n)),
        ],
        out_specs=pl.BlockSpec((4, tm, tn), lambda i, n: (0, i, n)),
        out_shape=jax.ShapeDtypeStruct((4, quarter, nc), BF16),
        compiler_params=_params(("parallel", "parallel")),
        name="seq_dft",
    )(cm_even, cm_odd, cm_odd, ab_even, ab_odd)
    return out.reshape(4 * quarter, nc)


def _layer1_out_kernel(x_ref, yc_ref, yd_ref, wout_ref, fg_ref, wg_ref, wu_ref, wd_ref, ng_ref, o_ref):
    y = jnp.concatenate([yc_ref[...], yd_ref[...]], axis=-1)
    x1 = x_ref[...] + _dot(y, wout_ref[...])
    o_ref[...] = _rms(_ffn(x1, _ffn_in(x1, fg_ref), wg_ref, wu_ref, wd_ref), ng_ref[...])


def _layer1_out_call(x, yc, yd, w_out, ffn_g, wg, wu, wd, final_g, ts):
    b, s, d = x.shape
    dc = yd.shape[2]
    return pl.pallas_call(
        _layer1_out_kernel,
        grid=(b, s // ts),
        in_specs=[
            pl.BlockSpec((None, ts, d), lambda bi, j: (bi, j, 0)),
            pl.BlockSpec((ts, dc), lambda bi, j: (j, bi)),
            pl.BlockSpec((None, ts, dc), lambda bi, j: (bi, j, 0)),
            _const_spec(w_out.shape),
            _const_spec((1, d)), _const_spec(wg.shape, 1), _const_spec(wu.shape, 1), _const_spec(wd.shape, 1),
            _const_spec((1, d)),
        ],
        out_specs=pl.BlockSpec((None, ts, d), lambda bi, j: (bi, j, 0)),
        out_shape=jax.ShapeDtypeStruct((b, s, d), F32),
        compiler_params=_params(("parallel", "parallel")),
        name="layer1_out",
    )(x, yc, yd, w_out, ffn_g.reshape(1, d), wg, wu, wd, final_g.reshape(1, d))


def _block_diag(w):
    g, n, _ = w.shape
    eye = jnp.eye(g, dtype=w.dtype)
    return (eye[:, None, :, None] * w[:, :, None, :]).reshape(g * n, g * n)


def _channel_dft_mats(n):
    idx = np.outer(np.arange(n), np.arange(n)) % n
    ang = 2.0 * np.pi * idx / n
    return np.stack([np.cos(ang), np.sin(ang)]).astype(np.float32)


def _fold_kernel(t_ref, w_ref, o_ref):
    n = t_ref.shape[1]
    o_ref[...] = jnp.zeros_like(o_ref)
    for q in range(2):
        for pi in range(o_ref.shape[1]):
            for sub in range(o_ref.shape[2] // n):
                blk = slice(sub * n, (sub + 1) * n)
                head = pi * (o_ref.shape[2] // n) + sub
                o_ref[q, pi, blk, blk] = jnp.dot(t_ref[q], w_ref[head], preferred_element_type=F32,
                                                 precision=lax.Precision.HIGHEST).astype(o_ref.dtype)


def _fold_channel_mix(fourier_w):
    heads, n, _ = fourier_w.shape
    per = MXU_DIM // n
    return pl.pallas_call(
        _fold_kernel,
        out_shape=jax.ShapeDtypeStruct((2, heads // per, per * n, per * n), BF16),
        name="fold_channel_mix",
    )(jnp.asarray(_channel_dft_mats(n)), fourier_w)


def _twiddle_kernel(ta_ref, tb_ref, o_ref):
    for blk in range(o_ref.shape[3] // LANES):
        ca = ta_ref[0, :, blk:blk + 1]
        sa = ta_ref[1, :, blk:blk + 1]
        cols = slice(blk * LANES, (blk + 1) * LANES)
        for pi in range(o_ref.shape[0]):
            cb = tb_ref[pi, 0]
            sb = tb_ref[pi, 1]
            o_ref[pi, 0, :, cols] = (ca * cb - sa * sb).astype(o_ref.dtype)
            o_ref[pi, 1, :, cols] = (-(sa * cb) - ca * sb).astype(o_ref.dtype)


def _seq_dft_mats(s, rows, parities):
    n_par = len(parities)
    half = s // 2
    nblk = half // LANES
    unit = 2.0 * math.pi / s

    def cos_sin(idx):
        ang = (idx % s).astype(F32) * unit
        return jnp.stack([jnp.cos(ang), jnp.sin(ang)])

    j = lax.broadcasted_iota(jnp.int32, (half, nblk), 0)
    blk = lax.broadcasted_iota(jnp.int32, (half, nblk), 1)
    ta = cos_sin(j * (2 * LANES * blk))
    j = lax.broadcasted_iota(jnp.int32, (half, LANES), 0)
    ml = lax.broadcasted_iota(jnp.int32, (half, LANES), 1)
    tb = jnp.stack([cos_sin(j * (2 * ml + parity)) for parity in parities])
    return pl.pallas_call(
        _twiddle_kernel,
        grid=(half // rows,),
        in_specs=[pl.BlockSpec((2, rows, nblk), lambda i: (0, i, 0)),
                  pl.BlockSpec((n_par, 2, rows, LANES), lambda i: (0, 0, i, 0))],
        out_specs=pl.BlockSpec((n_par, 2, rows, half), lambda i: (0, 0, i, 0)),
        out_shape=jax.ShapeDtypeStruct((n_par, 2, half, half), BF16),
        compiler_params=_params(("parallel",)),
        name=f"seq_dft_twiddles_{s}",
    )(ta, tb)


def kernel(x, mix_norm_g, ffn_norm_g, ev_w_in, ev_conv_w, ev_conv_b, ev_ln_g, ev_ln_b, ev_pool_w,
           ev_pool_scale, ev_w_out, od_w_in, od_fourier_w, od_v_ln_g, od_v_ln_b, od_spatial_w,
           od_spatial_b, od_w_out, ffn_w_gate, ffn_w_up, ffn_w_down, final_norm_g):
    b, s, d = x.shape
    assert mix_norm_g.shape[0] == 2, "one conv/pool layer followed by one fourier/gating layer"
    hc = od_fourier_w.shape[2]
    ts = 512
    ts1 = 1024

    cm_odd = _seq_dft_mats(s, 256, (1,))
    cm_even = _seq_dft_mats(s // 2, 256, (0, 1))
    dft_scale = 1.0 / math.sqrt(s * hc)
    wg, wu, wd = (w.astype(BF16) for w in (ffn_w_gate, ffn_w_up, ffn_w_down))

    x = _layer0_call(x.reshape(b * s, d), s, mix_norm_g[0], ev_w_in[0].astype(BF16), ev_conv_w[0],
                     ev_conv_b[0], ev_ln_g[0], ev_ln_b[0], _block_diag(ev_pool_w[0]).astype(BF16),
                     ev_pool_scale[0].reshape(-1), ev_w_out[0].astype(BF16),
                     ffn_norm_g[0], wg, wu, wd, ts).reshape(b, s, d)

    sb_full = jnp.broadcast_to(od_spatial_b[0].T[:, :, None], (CHUNK, N_HEADS, hc))
    ab_odd, ab_even, yd = _odd_in_call(
        x, mix_norm_g[1], od_w_in[0].astype(BF16), _fold_channel_mix(od_fourier_w[0]),
        od_v_ln_g[0].reshape(-1), od_v_ln_b[0].reshape(-1), od_spatial_w[0].astype(BF16),
        sb_full.reshape(CHUNK, N_HEADS * hc), ts1)
    yc = _dft_call(cm_even, cm_odd, ab_even, ab_odd, dft_scale, 512, 512)
    return _layer1_out_call(x, yc, yd, od_w_out[0].astype(BF16), ffn_norm_g[1], wg, wu, wd,
                            final_norm_g, ts1)
```

```python
import functools
import math

import numpy as np
import jax
import jax.numpy as jnp
from jax import lax
from jax.experimental import pallas as pl
from jax.experimental.pallas import tpu as pltpu

EPS = 1e-6
CONV_WIDTH = 31
CONV_PAD = CONV_WIDTH // 2
POOL_WINDOWS = (2, 4, 8, 16)
N_HEADS = 4
CHUNK = 128
HALO = 16
ROW_BLOCK = 64
LANE_BLOCK = 128
CONV_ROWS = 128
SUBLANES = 8
BF16_SUBLANES = 16
LANES = 128
MXU_DIM = 256
VMEM_LIMIT = 56 * 1024 * 1024

F32 = jnp.float32
BF16 = jnp.bfloat16


def _rms(xv, g):
    ms = jnp.mean(xv * xv, axis=-1, keepdims=True)
    return xv * lax.rsqrt(ms + EPS) * g


def _layernorm(xv, g, b):
    mu = jnp.mean(xv, axis=-1, keepdims=True)
    d = xv - mu
    var = jnp.mean(d * d, axis=-1, keepdims=True)
    return d * lax.rsqrt(var + EPS) * g + b


def _dot(a, b):
    return jnp.dot(a, b, preferred_element_type=F32)


def _const_spec(shape, layer=None):
    if layer is None:
        return pl.BlockSpec(shape, lambda *_: (0,) * len(shape), pipeline_mode=pl.Buffered(1))
    return pl.BlockSpec((None,) + tuple(shape[1:]), lambda *_: (layer,) + (0,) * (len(shape) - 1),
                        pipeline_mode=pl.Buffered(1))


def _params(semantics, flags=None):
    return pltpu.CompilerParams(dimension_semantics=semantics, vmem_limit_bytes=VMEM_LIMIT, flags=flags)


def _row_masks():
    sub = lax.broadcasted_iota(jnp.int32, (SUBLANES, LANES), 0)
    return {k: sub < SUBLANES - k for k in range(1, SUBLANES)}


def _shift_rows(a, k, keep):
    if k == 0:
        return a[:-SUBLANES]
    n, w = a.shape
    tiles = pltpu.roll(a.reshape(n // SUBLANES, SUBLANES, w), SUBLANES - k, axis=1)
    return jnp.concatenate([jnp.where(keep[k], tiles[t], tiles[t + 1]) for t in range(n // SUBLANES - 1)],
                           axis=0)


def _ffn_chunks(d_ff, rows):
    step = max(MXU_DIM, (1 << 19) // rows // MXU_DIM * MXU_DIM)
    return tuple((c, min(c + step, d_ff)) for c in range(0, d_ff, step))


def _ffn_in(x, g_ref):
    return _rms(x, g_ref[...]).astype(BF16)


def _ffn(x, h, wg_ref, wu_ref, wd_ref):
    acc = x
    for c0, c1 in _ffn_chunks(wg_ref.shape[1], x.shape[0]):
        gate = _dot(h, wg_ref[:, c0:c1])
        up = _dot(h, wu_ref[:, c0:c1])
        a = (gate * jax.nn.sigmoid(gate) * up).astype(BF16)
        acc = acc + _dot(a, wd_ref[c0:c1, :])
    return acc


def _even_mixer_branches(j, xm_ref, xp_ref, xn_ref, g_ref, win_ref, cw_ref, cb_ref, lg_ref, lb_ref,
                         h_s, u_s, p_s, c_s, y_s, pl_s, *, ts, seq):
    da = c_s.shape[1]
    gp = da // N_HEADS
    rows_in = ts + 2 * HALO
    keep = _row_masks()
    g = g_ref[...]
    h_s[0:HALO, :] = _rms(xp_ref[...], g).astype(BF16)
    h_s[HALO:HALO + ts, :] = _rms(xm_ref[...], g).astype(BF16)
    h_s[HALO + ts:, :] = _rms(xn_ref[...], g).astype(BF16)
    pos = j * ts - HALO + lax.broadcasted_iota(jnp.int32, (rows_in, 1), 0)
    valid = (pos >= 0) & (pos < seq)

    for l0 in range(0, da, LANE_BLOCK):
        lanes = slice(l0, l0 + LANE_BLOCK)
        z = _dot(h_s[...], win_ref[:, 2 * l0:2 * (l0 + LANE_BLOCK)])
        u = jnp.where(valid, z[:, :LANE_BLOCK] * jax.nn.sigmoid(z[:, LANE_BLOCK:]), 0.0)
        u_s[0, :, lanes] = u.astype(BF16)
        u_s[1, :, lanes] = jnp.concatenate([u[SUBLANES:], jnp.zeros((SUBLANES, LANE_BLOCK), F32)]).astype(BF16)
        for r0 in range(0, ts, CONV_ROWS):
            out = jnp.broadcast_to(cb_ref[:, lanes], (CONV_ROWS, LANE_BLOCK))
            for r in range(SUBLANES):
                v = None
                for q in range(pl.cdiv(CONV_WIDTH + HALO - CONV_PAD, SUBLANES)):
                    k = SUBLANES * q + r - (HALO - CONV_PAD)
                    if 0 <= k < CONV_WIDTH:
                        off = r0 + SUBLANES * (q - q % 2)
                        w_tap = jnp.tile(cw_ref[k, :, lanes], (CONV_ROWS // BF16_SUBLANES + 1, 1))
                        term = u_s[q % 2, off:off + CONV_ROWS + BF16_SUBLANES, lanes] * w_tap
                        v = term if v is None else v + term
                out = out + _shift_rows(v.astype(F32)[:CONV_ROWS + SUBLANES], r, keep)
            c_s[r0:r0 + CONV_ROWS, lanes] = out

    p_s[0:rows_in, :] = jnp.where(valid, _dot(h_s[...], win_ref[:, 2 * da:]), 0.0)
    for r0 in range(0, ts, ROW_BLOCK):
        ln = _layernorm(c_s[r0:r0 + ROW_BLOCK, :], lg_ref[...], lb_ref[...])
        y_s[r0:r0 + ROW_BLOCK, 0:da] = (ln * jax.nn.sigmoid(ln)).astype(BF16)
        t = j * ts + r0 + lax.broadcasted_iota(jnp.int32, (ROW_BLOCK, 1), 0)
        for gi, w in enumerate(POOL_WINDOWS):
            half = w // 2
            doublings = half.bit_length() - 1
            lanes = slice(gi * gp, (gi + 1) * gp)
            xw = p_s[r0 + HALO - SUBLANES:r0 + HALO + ROW_BLOCK + SUBLANES * doublings, lanes]
            d = xw
            for i in range(doublings):
                d = d[:-SUBLANES] + _shift_rows(d, 1 << i, keep)
            s = _shift_rows(d, SUBLANES - half, keep) + d[SUBLANES:]
            cnt = (jnp.minimum(t + half - 1, seq - 1) - jnp.maximum(t - half, 0) + 1).astype(F32)
            pooled = s / cnt - xw[SUBLANES:SUBLANES + ROW_BLOCK]
            pl_s[r0:r0 + ROW_BLOCK, lanes] = pooled.astype(BF16)


def _even_back(xm_ref, pw_ref, ps_ref, wout_ref, y_s, pl_s):
    da = pl_s.shape[1]
    y_s[:, da:] = (_dot(pl_s[...], pw_ref[...]) * ps_ref[...]).astype(BF16)
    return xm_ref[...] + _dot(y_s[...], wout_ref[...])


def _layer0_kernel(xm_ref, xp_ref, xn_ref, mg_ref, win_ref, cw_ref, cb_ref, lg_ref, lb_ref,
                   pw_ref, ps_ref, wout_ref, fg_ref, wg_ref, wu_ref, wd_ref, o_ref,
                   h_s, u_s, p_s, c_s, y_s, pl_s, carry_s, hcarry_s, *, ts, seq, n_tiles):
    g = pl.program_id(0)

    @pl.when(g == 0)
    def _():
        carry_s[...] = jnp.zeros_like(carry_s)
        hcarry_s[...] = jnp.zeros_like(hcarry_s)
        p_s[ts + 2 * HALO:, :] = jnp.zeros_like(p_s[ts + 2 * HALO:, :])

    j = jnp.minimum(g, n_tiles - 1) % (seq // ts)
    _even_mixer_branches(j, xm_ref, xp_ref, xn_ref, mg_ref, win_ref, cw_ref, cb_ref, lg_ref, lb_ref,
                         h_s, u_s, p_s, c_s, y_s, pl_s, ts=ts, seq=seq)
    o_ref[...] = _ffn(carry_s[...], hcarry_s[...], wg_ref, wu_ref, wd_ref)
    x1 = _even_back(xm_ref, pw_ref, ps_ref, wout_ref, y_s, pl_s)
    carry_s[...] = x1
    hcarry_s[...] = _ffn_in(x1, fg_ref)


def _layer0_call(x2d, seq, mix_g, w_in, conv_w, conv_b, ln_g, ln_b, pool_bd, pool_scale, w_out,
                 ffn_g, wg, wu, wd, ts):
    t, d = x2d.shape
    da = conv_w.shape[1]
    nh = ts // HALO
    n_tiles = t // ts
    last_halo = t // HALO - 1
    glu = w_in[:, :2 * da].reshape(d, 2, da // LANE_BLOCK, LANE_BLOCK).transpose(0, 2, 1, 3)
    w_in = jnp.concatenate([glu.reshape(d, 2 * da), w_in[:, 2 * da:]], axis=1)
    conv_w = jnp.broadcast_to(conv_w.astype(BF16)[:, None, :], (conv_w.shape[0], BF16_SUBLANES, da))

    def tile(g):
        return jnp.minimum(g, n_tiles - 1)

    kernel = functools.partial(_layer0_kernel, ts=ts, seq=seq, n_tiles=n_tiles)
    return pl.pallas_call(
        kernel,
        grid=(n_tiles + 1,),
        in_specs=[
            pl.BlockSpec((ts, d), lambda g: (tile(g), 0)),
            pl.BlockSpec((HALO, d), lambda g: (jnp.maximum(tile(g) * nh - 1, 0), 0)),
            pl.BlockSpec((HALO, d), lambda g: (jnp.minimum((tile(g) + 1) * nh, last_halo), 0)),
            _const_spec((1, d)), _const_spec(w_in.shape), _const_spec(conv_w.shape),
            _const_spec((1, da)), _const_spec((1, da)), _const_spec((1, da)),
            _const_spec(pool_bd.shape), _const_spec((1, da)), _const_spec(w_out.shape),
            _const_spec((1, d)), _const_spec(wg.shape, 0), _const_spec(wu.shape, 0), _const_spec(wd.shape, 0),
        ],
        out_specs=pl.BlockSpec((ts, d), lambda g: (jnp.maximum(g - 1, 0), 0)),
        out_shape=jax.ShapeDtypeStruct((t, d), F32),
        scratch_shapes=[
            pltpu.VMEM((ts + 2 * HALO, d), BF16),
            pltpu.VMEM((2, ts + 2 * HALO, da), BF16),
            pltpu.VMEM((ts + 2 * HALO + SUBLANES, da), F32),
            pltpu.VMEM((ts, da), F32),
            pltpu.VMEM((ts, d), BF16),
            pltpu.VMEM((ts, da), BF16),
            pltpu.VMEM((ts, d), F32),
            pltpu.VMEM((ts, d), BF16),
        ],
        compiler_params=_params(("arbitrary",)),
        name="layer0",
    )(x2d, x2d, x2d, mix_g.reshape(1, d), w_in, conv_w, conv_b.reshape(1, da), ln_g.reshape(1, da),
      ln_b.reshape(1, da), pool_bd, pool_scale.reshape(1, da), w_out, ffn_g.reshape(1, d), wg, wu, wd)


def _odd_in_kernel(x_ref, g_ref, win_ref, cm_ref, vg_ref, vb_ref, sw_ref, sb_ref,
                   abo_ref, abe_ref, yd_ref, ab_s, *, ts):
    dc = vg_ref.shape[1]
    hd = dc // N_HEADS
    pair = cm_ref.shape[2]
    h = _rms(x_ref[...], g_ref[...]).astype(BF16)
    z = _dot(h, win_ref[...])
    c = z[:, :dc].astype(BF16)
    for q in range(2):
        for pi in range(dc // pair):
            part = _dot(c[:, pi * pair:(pi + 1) * pair], cm_ref[q, pi])
            for sub in range(pair // LANES):
                blk = pi * (pair // LANES) + sub
                lanes = slice(blk * LANES, (blk + 1) * LANES)
                ab_s[blk] = part[:, sub * LANES:(sub + 1) * LANES]
                abo_ref[q, :, lanes] = ab_s[blk, pl.ds(1, ts // 2, stride=2), :].astype(BF16)
                for cls in range(2):
                    abe_ref[q, cls, :, lanes] = ab_s[blk, pl.ds(2 * cls, ts // 4, stride=4), :].astype(BF16)
    zuv = z[:, dc:]
    uv = 0.5 * zuv * (1.0 + lax.erf(zuv * math.sqrt(0.5)))
    u = uv[:, :dc]
    v = uv[:, dc:]
    for hi in range(N_HEADS):
        lanes = slice(hi * hd, (hi + 1) * hd)
        vn = _layernorm(v[:, lanes], vg_ref[:, lanes], vb_ref[:, lanes]).astype(BF16)
        for n in range(ts // CHUNK):
            rows = slice(n * CHUNK, (n + 1) * CHUNK)
            sv = _dot(sw_ref[hi], vn[rows, :]) + sb_ref[:, lanes]
            yd_ref[rows, lanes] = (u[rows, lanes] * sv).astype(BF16)


def _odd_in_call(x, g, w_in, cmix, v_ln_g, v_ln_b, spatial_w, sb_full, ts):
    b, s, d = x.shape
    dc = v_ln_g.shape[0]
    kernel = functools.partial(_odd_in_kernel, ts=ts)
    return pl.pallas_call(
        kernel,
        grid=(b, s // ts),
        in_specs=[
            pl.BlockSpec((None, ts, d), lambda bi, j: (bi, j, 0)),
            _const_spec((1, d)), _const_spec(w_in.shape), _const_spec(cmix.shape),
            _const_spec((1, dc)), _const_spec((1, dc)),
            _const_spec(spatial_w.shape), _const_spec(sb_full.shape),
        ],
        out_specs=[
            pl.BlockSpec((2, ts // 2, dc), lambda bi, j: (0, j, bi)),
            pl.BlockSpec((2, 2, ts // 4, dc), lambda bi, j: (0, 0, j, bi)),
            pl.BlockSpec((None, ts, dc), lambda bi, j: (bi, j, 0)),
        ],
        out_shape=[
            jax.ShapeDtypeStruct((2, s // 2, b * dc), BF16),
            jax.ShapeDtypeStruct((2, 2, s // 4, b * dc), BF16),
            jax.ShapeDtypeStruct((b, s, dc), BF16),
        ],
        scratch_shapes=[pltpu.VMEM((dc // LANES, ts, LANES), F32)],
        compiler_params=_params(("parallel", "parallel")),
        name="odd_mixer_in",
    )(x, g.reshape(1, d), w_in, cmix, v_ln_g.reshape(1, dc), v_ln_b.reshape(1, dc), spatial_w, sb_full)


def _dft_kernel(ce_ref, co_lo_ref, co_hi_ref, abe_ref, abo_ref, o_ref, *, scale):
    def cos_minus_sin(m_ref, a, b):
        return _dot(m_ref[0], a) + _dot(m_ref[1], b)

    ee = cos_minus_sin(ce_ref.at[0], abe_ref[0, 0], abe_ref[1, 0])
    eo = cos_minus_sin(ce_ref.at[1], abe_ref[0, 1], abe_ref[1, 1])
    o_lo = cos_minus_sin(co_lo_ref, abo_ref[0], abo_ref[1])
    o_hi = cos_minus_sin(co_hi_ref, abo_ref[0], abo_ref[1])
    e_lo = ee + eo
    e_hi = ee - eo
    for quarter, val in enumerate((e_lo + o_lo, e_hi + o_hi, e_lo - o_lo, e_hi - o_hi)):
        o_ref[quarter] = (val * scale).astype(o_ref.dtype)


def _dft_call(cm_even, cm_odd, ab_even, ab_odd, scale, tm, tn):
    _, half, nc = ab_odd.shape
    quarter = half // 2
    kernel = functools.partial(_dft_kernel, scale=scale)
    out = pl.pallas_call(
        kernel,
        grid=(quarter // tm, nc // tn),
        in_specs=[
            pl.BlockSpec((2, 2, tm, quarter), lambda i, n: (0, 0, i, 0)),
            pl.BlockSpec((None, 2, tm, half), lambda i, n: (0, 0, i, 0)),
            pl.BlockSpec((None, 2, tm, half), lambda i, n: (0, 0, quarter // tm + i, 0)),
            pl.BlockSpec((2, 2, quarter, tn), lambda i, n: (0, 0, 0, n)),
            pl.BlockSpec((2, half, tn), lambda i, n: (0, 0, n)),
        ],
        out_specs=pl.BlockSpec((4, tm, tn), lambda i, n: (0, i, n)),
        out_shape=jax.ShapeDtypeStruct((4, quarter, nc), BF16),
        compiler_params=_params(("parallel", "parallel")),
        name="seq_dft",
    )(cm_even, cm_odd, cm_odd, ab_even, ab_odd)
    return out.reshape(4 * quarter, nc)


def _layer1_out_kernel(x_ref, yc_ref, yd_ref, wout_ref, fg_ref, wg_ref, wu_ref, wd_ref, ng_ref, o_ref):
    y = jnp.concatenate([yc_ref[...], yd_ref[...]], axis=-1)
    x1 = x_ref[...] + _dot(y, wout_ref[...])
    o_ref[...] = _rms(_ffn(x1, _ffn_in(x1, fg_ref), wg_ref, wu_ref, wd_ref), ng_ref[...])


def _layer1_out_call(x, yc, yd, w_out, ffn_g, wg, wu, wd, final_g, ts):
    b, s, d = x.shape
    dc = yd.shape[2]
    return pl.pallas_call(
        _layer1_out_kernel,
        grid=(b, s // ts),
        in_specs=[
            pl.BlockSpec((None, ts, d), lambda bi, j: (bi, j, 0)),
            pl.BlockSpec((ts, dc), lambda bi, j: (j, bi)),
            pl.BlockSpec((None, ts, dc), lambda bi, j: (bi, j, 0)),
            _const_spec(w_out.shape),
            _const_spec((1, d)), _const_spec(wg.shape, 1), _const_spec(wu.shape, 1), _const_spec(wd.shape, 1),
            _const_spec((1, d)),
        ],
        out_specs=pl.BlockSpec((None, ts, d), lambda bi, j: (bi, j, 0)),
        out_shape=jax.ShapeDtypeStruct((b, s, d), F32),
        compiler_params=_params(("parallel", "parallel")),
        name="layer1_out",
    )(x, yc, yd, w_out, ffn_g.reshape(1, d), wg, wu, wd, final_g.reshape(1, d))


def _block_diag(w):
    g, n, _ = w.shape
    eye = jnp.eye(g, dtype=w.dtype)
    return (eye[:, None, :, None] * w[:, :, None, :]).reshape(g * n, g * n)


def _channel_dft_mats(n):
    idx = np.outer(np.arange(n), np.arange(n)) % n
    ang = 2.0 * np.pi * idx / n
    return np.stack([np.cos(ang), np.sin(ang)]).astype(np.float32)


def _fold_kernel(t_ref, w_ref, o_ref):
    n = t_ref.shape[1]
    o_ref[...] = jnp.zeros_like(o_ref)
    for q in range(2):
        for pi in range(o_ref.shape[1]):
            for sub in range(o_ref.shape[2] // n):
                blk = slice(sub * n, (sub + 1) * n)
                head = pi * (o_ref.shape[2] // n) + sub
                o_ref[q, pi, blk, blk] = jnp.dot(t_ref[q], w_ref[head], preferred_element_type=F32,
                                                 precision=lax.Precision.HIGHEST).astype(o_ref.dtype)


def _fold_channel_mix(fourier_w):
    heads, n, _ = fourier_w.shape
    per = MXU_DIM // n
    return pl.pallas_call(
        _fold_kernel,
        out_shape=jax.ShapeDtypeStruct((2, heads // per, per * n, per * n), BF16),
        name="fold_channel_mix",
    )(jnp.asarray(_channel_dft_mats(n)), fourier_w)


def _twiddle_kernel(ta_ref, tb_ref, o_ref):
    for blk in range(o_ref.shape[3] // LANES):
        ca = ta_ref[0, :, blk:blk + 1]
        sa = ta_ref[1, :, blk:blk + 1]
        cols = slice(blk * LANES, (blk + 1) * LANES)
        for pi in range(o_ref.shape[0]):
            cb = tb_ref[pi, 0]
            sb = tb_ref[pi, 1]
            o_ref[pi, 0, :, cols] = (ca * cb - sa * sb).astype(o_ref.dtype)
            o_ref[pi, 1, :, cols] = (-(sa * cb) - ca * sb).astype(o_ref.dtype)


def _seq_dft_mats(s, rows, parities):
    n_par = len(parities)
    half = s // 2
    nblk = half // LANES
    unit = 2.0 * math.pi / s

    def cos_sin(idx):
        ang = (idx % s).astype(F32) * unit
        return jnp.stack([jnp.cos(ang), jnp.sin(ang)])

    j = lax.broadcasted_iota(jnp.int32, (half, nblk), 0)
    blk = lax.broadcasted_iota(jnp.int32, (half, nblk), 1)
    ta = cos_sin(j * (2 * LANES * blk))
    j = lax.broadcasted_iota(jnp.int32, (half, LANES), 0)
    ml = lax.broadcasted_iota(jnp.int32, (half, LANES), 1)
    tb = jnp.stack([cos_sin(j * (2 * ml + parity)) for parity in parities])
    return pl.pallas_call(
        _twiddle_kernel,
        grid=(half // rows,),
        in_specs=[pl.BlockSpec((2, rows, nblk), lambda i: (0, i, 0)),
                  pl.BlockSpec((n_par, 2, rows, LANES), lambda i: (0, 0, i, 0))],
        out_specs=pl.BlockSpec((n_par, 2, rows, half), lambda i: (0, 0, i, 0)),
        out_shape=jax.ShapeDtypeStruct((n_par, 2, half, half), BF16),
        compiler_params=_params(("parallel",)),
        name=f"seq_dft_twiddles_{s}",
    )(ta, tb)


def kernel(x, mix_norm_g, ffn_norm_g, ev_w_in, ev_conv_w, ev_conv_b, ev_ln_g, ev_ln_b, ev_pool_w,
           ev_pool_scale, ev_w_out, od_w_in, od_fourier_w, od_v_ln_g, od_v_ln_b, od_spatial_w,
           od_spatial_b, od_w_out, ffn_w_gate, ffn_w_up, ffn_w_down, final_norm_g):
    b, s, d = x.shape
    assert mix_norm_g.shape[0] == 2, "one conv/pool layer followed by one fourier/gating layer"
    hc = od_fourier_w.shape[2]
    ts = 512
    ts1 = 1024

    cm_odd = _seq_dft_mats(s, 256, (1,))
    cm_even = _seq_dft_mats(s // 2, 256, (0, 1))
    dft_scale = 1.0 / math.sqrt(s * hc)
    wg, wu, wd = (w.astype(BF16) for w in (ffn_w_gate, ffn_w_up, ffn_w_down))

    x = _layer0_call(x.reshape(b * s, d), s, mix_norm_g[0], ev_w_in[0].astype(BF16), ev_conv_w[0],
                     ev_conv_b[0], ev_ln_g[0], ev_ln_b[0], _block_diag(ev_pool_w[0]).astype(BF16),
                     ev_pool_scale[0].reshape(-1), ev_w_out[0].astype(BF16),
                     ffn_norm_g[0], wg, wu, wd, ts).reshape(b, s, d)

    sb_full = jnp.broadcast_to(od_spatial_b[0].T[:, :, None], (CHUNK, N_HEADS, hc))
    ab_odd, ab_even, yd = _odd_in_call(
        x, mix_norm_g[1], od_w_in[0].astype(BF16), _fold_channel_mix(od_fourier_w[0]),
        od_v_ln_g[0].reshape(-1), od_v_ln_b[0].reshape(-1), od_spatial_w[0].astype(BF16),
        sb_full.reshape(CHUNK, N_HEADS * hc), ts1)
    yc = _dft_call(cm_even, cm_odd, ab_even, ab_odd, dft_scale, 512, 512)
    return _layer1_out_call(x, yc, yd, od_w_out[0].astype(BF16), ffn_norm_g[1], wg, wu, wd,
                            final_norm_g, ts1)
```

```python
import functools
import math

import numpy as np
import jax
import jax.numpy as jnp
from jax import lax
from jax.experimental import pallas as pl
from jax.experimental.pallas import tpu as pltpu

EPS = 1e-6
CONV_WIDTH = 31
CONV_PAD = CONV_WIDTH // 2
POOL_WINDOWS = (2, 4, 8, 16)
N_HEADS = 4
CHUNK = 128
HALO = 16
ROW_BLOCK = 64
LANE_BLOCK = 128
FFN_CHUNKS_AFTER_BLOCK = (0, 0, 2, 0)
SUBLANES = 8
LANES = 128
MXU_DIM = 256
VMEM_LIMIT = 56 * 1024 * 1024

F32 = jnp.float32
BF16 = jnp.bfloat16


def _rms(xv, g):
    ms = jnp.mean(xv * xv, axis=-1, keepdims=True)
    return xv * lax.rsqrt(ms + EPS) * g


def _layernorm(xv, g, b):
    mu = jnp.mean(xv, axis=-1, keepdims=True)
    d = xv - mu
    var = jnp.mean(d * d, axis=-1, keepdims=True)
    return d * lax.rsqrt(var + EPS) * g + b


def _dot(a, b):
    return jnp.dot(a, b, preferred_element_type=F32)


def _const_spec(shape, layer=None):
    if layer is None:
        return pl.BlockSpec(shape, lambda *_: (0,) * len(shape), pipeline_mode=pl.Buffered(1))
    return pl.BlockSpec((None,) + tuple(shape[1:]), lambda *_: (layer,) + (0,) * (len(shape) - 1),
                        pipeline_mode=pl.Buffered(1))


def _params(semantics, flags=None):
    return pltpu.CompilerParams(dimension_semantics=semantics, vmem_limit_bytes=VMEM_LIMIT, flags=flags)


def _row_masks():
    sub = lax.broadcasted_iota(jnp.int32, (SUBLANES, LANES), 0)
    return {k: sub < SUBLANES - k for k in range(1, SUBLANES)}


def _shift_rows(a, k, keep):
    if k == 0:
        return a[:-SUBLANES]
    n, w = a.shape
    tiles = pltpu.roll(a.reshape(n // SUBLANES, SUBLANES, w), SUBLANES - k, axis=1)
    return jnp.concatenate([jnp.where(keep[k], tiles[t], tiles[t + 1]) for t in range(n // SUBLANES - 1)],
                           axis=0)


def _ffn_chunks(d_ff, rows):
    step = max(MXU_DIM, (1 << 19) // rows // MXU_DIM * MXU_DIM)
    return tuple((c, min(c + step, d_ff)) for c in range(0, d_ff, step))


def _ffn_in(x, g_ref):
    return _rms(x, g_ref[...]).astype(BF16)


def _ffn_partial_sums(x, h, wg_ref, wu_ref, wd_ref, rows=None):
    acc = x
    for c0, c1 in _ffn_chunks(wg_ref.shape[1], rows or x.shape[0]):
        gate = _dot(h, wg_ref[:, c0:c1])
        up = _dot(h, wu_ref[:, c0:c1])
        a = (gate * jax.nn.sigmoid(gate) * up).astype(BF16)
        acc = acc + _dot(a, wd_ref[c0:c1, :])
        yield acc


def _ffn(x, h, wg_ref, wu_ref, wd_ref):
    *_, out = _ffn_partial_sums(x, h, wg_ref, wu_ref, wd_ref)
    return out


def _even_mixer_branches(j, xm_ref, xp_ref, xn_ref, g_ref, win_ref, cw_ref, cb_ref, lg_ref, lb_ref,
                         h_s, u_s, p_s, c_s, y_s, pl_s, *, ts, seq, after_block=lambda: None):
    da = u_s.shape[1]
    gp = da // N_HEADS
    rows_in = ts + 2 * HALO
    keep = _row_masks()
    g = g_ref[...]
    h_s[0:HALO, :] = _rms(xp_ref[...], g).astype(BF16)
    h_s[HALO:HALO + ts, :] = _rms(xm_ref[...], g).astype(BF16)
    h_s[HALO + ts:, :] = _rms(xn_ref[...], g).astype(BF16)
    pos = j * ts - HALO + lax.broadcasted_iota(jnp.int32, (rows_in, 1), 0)
    valid = (pos >= 0) & (pos < seq)

    for l0 in range(0, da, LANE_BLOCK):
        lanes = slice(l0, l0 + LANE_BLOCK)
        z = _dot(h_s[...], win_ref[:, 2 * l0:2 * (l0 + LANE_BLOCK)])
        u_s[:, lanes] = jnp.where(valid, z[:, :LANE_BLOCK] * jax.nn.sigmoid(z[:, LANE_BLOCK:]), 0.0)
        for r0 in range(0, ts, ROW_BLOCK):
            out = jnp.broadcast_to(cb_ref[:, lanes], (ROW_BLOCK, LANE_BLOCK))
            for r in range(SUBLANES):
                v = None
                for q in range(pl.cdiv(CONV_WIDTH + HALO - CONV_PAD, SUBLANES)):
                    k = SUBLANES * q + r - (HALO - CONV_PAD)
                    if 0 <= k < CONV_WIDTH:
                        term = (u_s[r0 + SUBLANES * q:r0 + SUBLANES * q + ROW_BLOCK + SUBLANES, lanes]
                                * cw_ref[k:k + 1, lanes])
                        v = term if v is None else v + term
                out = out + _shift_rows(v, r, keep)
            c_s[r0:r0 + ROW_BLOCK, lanes] = out
        after_block()

    p_s[0:rows_in, :] = jnp.where(valid, _dot(h_s[...], win_ref[:, 2 * da:]), 0.0)
    for r0 in range(0, ts, ROW_BLOCK):
        ln = _layernorm(c_s[r0:r0 + ROW_BLOCK, :], lg_ref[...], lb_ref[...])
        y_s[r0:r0 + ROW_BLOCK, 0:da] = (ln * jax.nn.sigmoid(ln)).astype(BF16)
        t = j * ts + r0 + lax.broadcasted_iota(jnp.int32, (ROW_BLOCK, 1), 0)
        for gi, w in enumerate(POOL_WINDOWS):
            half = w // 2
            doublings = half.bit_length() - 1
            lanes = slice(gi * gp, (gi + 1) * gp)
            xw = p_s[r0 + HALO - SUBLANES:r0 + HALO + ROW_BLOCK + SUBLANES * doublings, lanes]
            d = xw
            for i in range(doublings):
                d = d[:-SUBLANES] + _shift_rows(d, 1 << i, keep)
            s = _shift_rows(d, SUBLANES - half, keep) + d[SUBLANES:]
            cnt = (jnp.minimum(t + half - 1, seq - 1) - jnp.maximum(t - half, 0) + 1).astype(F32)
            pooled = s / cnt - xw[SUBLANES:SUBLANES + ROW_BLOCK]
            pl_s[r0:r0 + ROW_BLOCK, lanes] = pooled.astype(BF16)


def _even_back(xm_ref, pw_ref, ps_ref, wout_ref, y_s, pl_s):
    da = pl_s.shape[1]
    y_s[:, da:] = (_dot(pl_s[...], pw_ref[...]) * ps_ref[...]).astype(BF16)
    return xm_ref[...] + _dot(y_s[...], wout_ref[...])


def _layer0_kernel(xm_ref, xp_ref, xn_ref, mg_ref, win_ref, cw_ref, cb_ref, lg_ref, lb_ref,
                   pw_ref, ps_ref, wout_ref, fg_ref, wg_ref, wu_ref, wd_ref, o_ref,
                   h_s, u_s, p_s, c_s, y_s, pl_s, carry_s, hcarry_s, xprev_s, hprev_s, *, ts, seq, n_tiles):
    g = pl.program_id(0)

    @pl.when(g == 0)
    def _():
        carry_s[...] = jnp.zeros_like(carry_s)
        hcarry_s[...] = jnp.zeros_like(hcarry_s)
        p_s[ts + 2 * HALO:, :] = jnp.zeros_like(p_s[ts + 2 * HALO:, :])

    j = jnp.minimum(g, n_tiles - 1) % (seq // ts)
    xprev_s[...] = carry_s[...]
    hprev_s[...] = hcarry_s[...]
    partial = _ffn_partial_sums(xprev_s[...], hprev_s[...], wg_ref, wu_ref, wd_ref)
    sums = []
    per_block = iter(FFN_CHUNKS_AFTER_BLOCK)

    def ffn_chunk():
        for _ in range(next(per_block, 0)):
            sums.append(next(partial))

    _even_mixer_branches(j, xm_ref, xp_ref, xn_ref, mg_ref, win_ref, cw_ref, cb_ref, lg_ref, lb_ref,
                         h_s, u_s, p_s, c_s, y_s, pl_s, ts=ts, seq=seq, after_block=ffn_chunk)
    sums.extend(partial)
    o_ref[...] = sums[-1]
    x1 = _even_back(xm_ref, pw_ref, ps_ref, wout_ref, y_s, pl_s)
    carry_s[...] = x1
    hcarry_s[...] = _ffn_in(x1, fg_ref)


def _layer0_call(x2d, seq, mix_g, w_in, conv_w, conv_b, ln_g, ln_b, pool_bd, pool_scale, w_out,
                 ffn_g, wg, wu, wd, ts):
    t, d = x2d.shape
    da = conv_w.shape[1]
    nh = ts // HALO
    n_tiles = t // ts
    last_halo = t // HALO - 1
    glu = w_in[:, :2 * da].reshape(d, 2, da // LANE_BLOCK, LANE_BLOCK).transpose(0, 2, 1, 3)
    w_in = jnp.concatenate([glu.reshape(d, 2 * da), w_in[:, 2 * da:]], axis=1)

    def tile(g):
        return jnp.minimum(g, n_tiles - 1)

    kernel = functools.partial(_layer0_kernel, ts=ts, seq=seq, n_tiles=n_tiles)
    return pl.pallas_call(
        kernel,
        grid=(n_tiles + 1,),
        in_specs=[
            pl.BlockSpec((ts, d), lambda g: (tile(g), 0)),
            pl.BlockSpec((HALO, d), lambda g: (jnp.maximum(tile(g) * nh - 1, 0), 0)),
            pl.BlockSpec((HALO, d), lambda g: (jnp.minimum((tile(g) + 1) * nh, last_halo), 0)),
            _const_spec((1, d)), _const_spec(w_in.shape), _const_spec(conv_w.shape),
            _const_spec((1, da)), _const_spec((1, da)), _const_spec((1, da)),
            _const_spec(pool_bd.shape), _const_spec((1, da)), _const_spec(w_out.shape),
            _const_spec((1, d)), _const_spec(wg.shape, 0), _const_spec(wu.shape, 0), _const_spec(wd.shape, 0),
        ],
        out_specs=pl.BlockSpec((ts, d), lambda g: (jnp.maximum(g - 1, 0), 0)),
        out_shape=jax.ShapeDtypeStruct((t, d), F32),
        scratch_shapes=[
            pltpu.VMEM((ts + 2 * HALO, d), BF16),
            pltpu.VMEM((ts + 2 * HALO, da), F32),
            pltpu.VMEM((ts + 2 * HALO + SUBLANES, da), F32),
            pltpu.VMEM((ts, da), F32),
            pltpu.VMEM((ts, d), BF16),
            pltpu.VMEM((ts, da), BF16),
            pltpu.VMEM((ts, d), F32),
            pltpu.VMEM((ts, d), BF16),
            pltpu.VMEM((ts, d), F32),
            pltpu.VMEM((ts, d), BF16),
        ],
        compiler_params=_params(("arbitrary",)),
        name="layer0",
    )(x2d, x2d, x2d, mix_g.reshape(1, d), w_in, conv_w, conv_b.reshape(1, da), ln_g.reshape(1, da),
      ln_b.reshape(1, da), pool_bd, pool_scale.reshape(1, da), w_out, ffn_g.reshape(1, d), wg, wu, wd)


def _odd_in_kernel(x_ref, g_ref, win_ref, cm_ref, vg_ref, vb_ref, sw_ref, sb_ref,
                   abo_ref, abe_ref, yd_ref, ab_s, *, ts):
    dc = vg_ref.shape[1]
    hd = dc // N_HEADS
    pair = cm_ref.shape[2]
    h = _rms(x_ref[...], g_ref[...]).astype(BF16)
    z = _dot(h, win_ref[...])
    c = z[:, :dc].astype(BF16)
    for q in range(2):
        for pi in range(dc // pair):
            part = _dot(c[:, pi * pair:(pi + 1) * pair], cm_ref[q, pi])
            for sub in range(pair // LANES):
                blk = pi * (pair // LANES) + sub
                lanes = slice(blk * LANES, (blk + 1) * LANES)
                ab_s[blk] = part[:, sub * LANES:(sub + 1) * LANES]
                abo_ref[q, :, lanes] = ab_s[blk, pl.ds(1, ts // 2, stride=2), :].astype(BF16)
                for cls in range(2):
                    abe_ref[q, cls, :, lanes] = ab_s[blk, pl.ds(2 * cls, ts // 4, stride=4), :].astype(BF16)
    zuv = z[:, dc:]
    uv = 0.5 * zuv * (1.0 + lax.erf(zuv * math.sqrt(0.5)))
    u = uv[:, :dc]
    v = uv[:, dc:]
    for hi in range(N_HEADS):
        lanes = slice(hi * hd, (hi + 1) * hd)
        vn = _layernorm(v[:, lanes], vg_ref[:, lanes], vb_ref[:, lanes]).astype(BF16)
        for n in range(ts // CHUNK):
            rows = slice(n * CHUNK, (n + 1) * CHUNK)
            sv = _dot(sw_ref[hi], vn[rows, :]) + sb_ref[:, lanes]
            yd_ref[rows, lanes] = (u[rows, lanes] * sv).astype(BF16)


def _odd_in_call(x, g, w_in, cmix, v_ln_g, v_ln_b, spatial_w, sb_full, ts):
    b, s, d = x.shape
    dc = v_ln_g.shape[0]
    kernel = functools.partial(_odd_in_kernel, ts=ts)
    return pl.pallas_call(
        kernel,
        grid=(b, s // ts),
        in_specs=[
            pl.BlockSpec((None, ts, d), lambda bi, j: (bi, j, 0)),
            _const_spec((1, d)), _const_spec(w_in.shape), _const_spec(cmix.shape),
            _const_spec((1, dc)), _const_spec((1, dc)),
            _const_spec(spatial_w.shape), _const_spec(sb_full.shape),
        ],
        out_specs=[
            pl.BlockSpec((2, ts // 2, dc), lambda bi, j: (0, j, bi)),
            pl.BlockSpec((2, 2, ts // 4, dc), lambda bi, j: (0, 0, j, bi)),
            pl.BlockSpec((None, ts, dc), lambda bi, j: (bi, j, 0)),
        ],
        out_shape=[
            jax.ShapeDtypeStruct((2, s // 2, b * dc), BF16),
            jax.ShapeDtypeStruct((2, 2, s // 4, b * dc), BF16),
            jax.ShapeDtypeStruct((b, s, dc), BF16),
        ],
        scratch_shapes=[pltpu.VMEM((dc // LANES, ts, LANES), F32)],
        compiler_params=_params(("parallel", "parallel")),
        name="odd_mixer_in",
    )(x, g.reshape(1, d), w_in, cmix, v_ln_g.reshape(1, dc), v_ln_b.reshape(1, dc), spatial_w, sb_full)


def _dft_kernel(ce_ref, co_lo_ref, co_hi_ref, abe_ref, abo_ref, o_ref, *, scale):
    def cos_minus_sin(m_ref, a, b):
        return _dot(m_ref[0], a) + _dot(m_ref[1], b)

    ee = cos_minus_sin(ce_ref.at[0], abe_ref[0, 0], abe_ref[1, 0])
    eo = cos_minus_sin(ce_ref.at[1], abe_ref[0, 1], abe_ref[1, 1])
    o_lo = cos_minus_sin(co_lo_ref, abo_ref[0], abo_ref[1])
    o_hi = cos_minus_sin(co_hi_ref, abo_ref[0], abo_ref[1])
    e_lo = ee + eo
    e_hi = ee - eo
    for quarter, val in enumerate((e_lo + o_lo, e_hi + o_hi, e_lo - o_lo, e_hi - o_hi)):
        o_ref[quarter] = (val * scale).astype(o_ref.dtype)


def _dft_call(cm_even, cm_odd, ab_even, ab_odd, scale, tm, tn):
    _, half, nc = ab_odd.shape
    quarter = half // 2
    kernel = functools.partial(_dft_kernel, scale=scale)
    out = pl.pallas_call(
        kernel,
        grid=(quarter // tm, nc // tn),
        in_specs=[
            pl.BlockSpec((2, 2, tm, quarter), lambda i, n: (0, 0, i, 0)),
            pl.BlockSpec((None, 2, tm, half), lambda i, n: (0, 0, i, 0)),
            pl.BlockSpec((None, 2, tm, half), lambda i, n: (0, 0, quarter // tm + i, 0)),
            pl.BlockSpec((2, 2, quarter, tn), lambda i, n: (0, 0, 0, n)),
            pl.BlockSpec((2, half, tn), lambda i, n: (0, 0, n)),
        ],
        out_specs=pl.BlockSpec((4, tm, tn), lambda i, n: (0, i, n)),
        out_shape=jax.ShapeDtypeStruct((4, quarter, nc), BF16),
        compiler_params=_params(("parallel", "parallel")),
        name="seq_dft",
    )(cm_even, cm_odd, cm_odd, ab_even, ab_odd)
    return out.reshape(4 * quarter, nc)


def _layer1_out_kernel(x_ref, yc_ref, yd_ref, wout_ref, fg_ref, wg_ref, wu_ref, wd_ref, ng_ref, o_ref):
    y = jnp.concatenate([yc_ref[...], yd_ref[...]], axis=-1)
    x1 = x_ref[...] + _dot(y, wout_ref[...])
    o_ref[...] = _rms(_ffn(x1, _ffn_in(x1, fg_ref), wg_ref, wu_ref, wd_ref), ng_ref[...])


def _layer1_out_call(x, yc, yd, w_out, ffn_g, wg, wu, wd, final_g, ts):
    b, s, d = x.shape
    dc = yd.shape[2]
    return pl.pallas_call(
        _layer1_out_kernel,
        grid=(b, s // ts),
        in_specs=[
            pl.BlockSpec((None, ts, d), lambda bi, j: (bi, j, 0)),
            pl.BlockSpec((ts, dc), lambda bi, j: (j, bi)),
            pl.BlockSpec((None, ts, dc), lambda bi, j: (bi, j, 0)),
            _const_spec(w_out.shape),
            _const_spec((1, d)), _const_spec(wg.shape, 1), _const_spec(wu.shape, 1), _const_spec(wd.shape, 1),
            _const_spec((1, d)),
        ],
        out_specs=pl.BlockSpec((None, ts, d), lambda bi, j: (bi, j, 0)),
        out_shape=jax.ShapeDtypeStruct((b, s, d), F32),
        compiler_params=_params(("parallel", "parallel")),
        name="layer1_out",
    )(x, yc, yd, w_out, ffn_g.reshape(1, d), wg, wu, wd, final_g.reshape(1, d))


def _block_diag(w):
    g, n, _ = w.shape
    eye = jnp.eye(g, dtype=w.dtype)
    return (eye[:, None, :, None] * w[:, :, None, :]).reshape(g * n, g * n)


def _channel_dft_mats(n):
    idx = np.outer(np.arange(n), np.arange(n)) % n
    ang = 2.0 * np.pi * idx / n
    return np.stack([np.cos(ang), np.sin(ang)]).astype(np.float32)


def _fold_kernel(t_ref, w_ref, o_ref):
    n = t_ref.shape[1]
    o_ref[...] = jnp.zeros_like(o_ref)
    for q in range(2):
        for pi in range(o_ref.shape[1]):
            for sub in range(o_ref.shape[2] // n):
                blk = slice(sub * n, (sub + 1) * n)
                head = pi * (o_ref.shape[2] // n) + sub
                o_ref[q, pi, blk, blk] = jnp.dot(t_ref[q], w_ref[head], preferred_element_type=F32,
                                                 precision=lax.Precision.HIGHEST).astype(o_ref.dtype)


def _fold_channel_mix(fourier_w):
    heads, n, _ = fourier_w.shape
    per = MXU_DIM // n
    return pl.pallas_call(
        _fold_kernel,
        out_shape=jax.ShapeDtypeStruct((2, heads // per, per * n, per * n), BF16),
        name="fold_channel_mix",
    )(jnp.asarray(_channel_dft_mats(n)), fourier_w)


def _twiddle_kernel(ta_ref, tb_ref, o_ref):
    for blk in range(o_ref.shape[3] // LANES):
        ca = ta_ref[0, :, blk:blk + 1]
        sa = ta_ref[1, :, blk:blk + 1]
        cols = slice(blk * LANES, (blk + 1) * LANES)
        for pi in range(o_ref.shape[0]):
            cb = tb_ref[pi, 0]
            sb = tb_ref[pi, 1]
            o_ref[pi, 0, :, cols] = (ca * cb - sa * sb).astype(o_ref.dtype)
            o_ref[pi, 1, :, cols] = (-(sa * cb) - ca * sb).astype(o_ref.dtype)


def _seq_dft_mats(s, rows, parities):
    n_par = len(parities)
    half = s // 2
    nblk = half // LANES
    unit = 2.0 * math.pi / s

    def cos_sin(idx):
        ang = (idx % s).astype(F32) * unit
        return jnp.stack([jnp.cos(ang), jnp.sin(ang)])

    j = lax.broadcasted_iota(jnp.int32, (half, nblk), 0)
    blk = lax.broadcasted_iota(jnp.int32, (half, nblk), 1)
    ta = cos_sin(j * (2 * LANES * blk))
    j = lax.broadcasted_iota(jnp.int32, (half, LANES), 0)
    ml = lax.broadcasted_iota(jnp.int32, (half, LANES), 1)
    tb = jnp.stack([cos_sin(j * (2 * ml + parity)) for parity in parities])
    return pl.pallas_call(
        _twiddle_kernel,
        grid=(half // rows,),
        in_specs=[pl.BlockSpec((2, rows, nblk), lambda i: (0, i, 0)),
                  pl.BlockSpec((n_par, 2, rows, LANES), lambda i: (0, 0, i, 0))],
        out_specs=pl.BlockSpec((n_par, 2, rows, half), lambda i: (0, 0, i, 0)),
        out_shape=jax.ShapeDtypeStruct((n_par, 2, half, half), BF16),
        compiler_params=_params(("parallel",)),
        name=f"seq_dft_twiddles_{s}",
    )(ta, tb)


def kernel(x, mix_norm_g, ffn_norm_g, ev_w_in, ev_conv_w, ev_conv_b, ev_ln_g, ev_ln_b, ev_pool_w,
           ev_pool_scale, ev_w_out, od_w_in, od_fourier_w, od_v_ln_g, od_v_ln_b, od_spatial_w,
           od_spatial_b, od_w_out, ffn_w_gate, ffn_w_up, ffn_w_down, final_norm_g):
    b, s, d = x.shape
    assert mix_norm_g.shape[0] == 2, "one conv/pool layer followed by one fourier/gating layer"
    hc = od_fourier_w.shape[2]
    ts = 512
    ts1 = 1024

    cm_odd = _seq_dft_mats(s, 256, (1,))
    cm_even = _seq_dft_mats(s // 2, 256, (0, 1))
    dft_scale = 1.0 / math.sqrt(s * hc)
    wg, wu, wd = (w.astype(BF16) for w in (ffn_w_gate, ffn_w_up, ffn_w_down))

    x = _layer0_call(x.reshape(b * s, d), s, mix_norm_g[0], ev_w_in[0].astype(BF16), ev_conv_w[0],
                     ev_conv_b[0], ev_ln_g[0], ev_ln_b[0], _block_diag(ev_pool_w[0]).astype(BF16),
                     ev_pool_scale[0].reshape(-1), ev_w_out[0].astype(BF16),
                     ffn_norm_g[0], wg, wu, wd, ts).reshape(b, s, d)

    sb_full = jnp.broadcast_to(od_spatial_b[0].T[:, :, None], (CHUNK, N_HEADS, hc))
    ab_odd, ab_even, yd = _odd_in_call(
        x, mix_norm_g[1], od_w_in[0].astype(BF16), _fold_channel_mix(od_fourier_w[0]),
        od_v_ln_g[0].reshape(-1), od_v_ln_b[0].reshape(-1), od_spatial_w[0].astype(BF16),
        sb_full.reshape(CHUNK, N_HEADS * hc), ts1)
    yc = _dft_call(cm_even, cm_odd, ab_even, ab_odd, dft_scale, 512, 512)
    return _layer1_out_call(x, yc, yd, od_w_out[0].astype(BF16), ffn_norm_g[1], wg, wu, wd,
                            final_norm_g, ts1)
```

```python
import functools
import math

import numpy as np
import jax
import jax.numpy as jnp
from jax import lax
from jax.experimental import pallas as pl
from jax.experimental.pallas import tpu as pltpu

EPS = 1e-6
CONV_WIDTH = 31
CONV_PAD = CONV_WIDTH // 2
POOL_WINDOWS = (2, 4, 8, 16)
N_HEADS = 4
CHUNK = 128
HALO = 16
ROW_BLOCK = 64
LANE_BLOCK = 128
FFN_CHUNKS_AFTER_BLOCK = (0, 0, 2, 0)
SUBLANES = 8
LANES = 128
MXU_DIM = 256
VMEM_LIMIT = 56 * 1024 * 1024

F32 = jnp.float32
BF16 = jnp.bfloat16


def _rms(xv, g):
    ms = jnp.mean(xv * xv, axis=-1, keepdims=True)
    return xv * lax.rsqrt(ms + EPS) * g


def _layernorm(xv, g, b):
    mu = jnp.mean(xv, axis=-1, keepdims=True)
    d = xv - mu
    var = jnp.mean(d * d, axis=-1, keepdims=True)
    return d * lax.rsqrt(var + EPS) * g + b


def _dot(a, b):
    return jnp.dot(a, b, preferred_element_type=F32)


def _const_spec(shape, layer=None):
    if layer is None:
        return pl.BlockSpec(shape, lambda *_: (0,) * len(shape), pipeline_mode=pl.Buffered(1))
    return pl.BlockSpec((None,) + tuple(shape[1:]), lambda *_: (layer,) + (0,) * (len(shape) - 1),
                        pipeline_mode=pl.Buffered(1))


def _params(semantics):
    return pltpu.CompilerParams(dimension_semantics=semantics, vmem_limit_bytes=VMEM_LIMIT)


def _row_masks():
    sub = lax.broadcasted_iota(jnp.int32, (SUBLANES, LANES), 0)
    return {k: sub < SUBLANES - k for k in range(1, SUBLANES)}


def _shift_rows(a, k, keep):
    if k == 0:
        return a[:-SUBLANES]
    n, w = a.shape
    tiles = pltpu.roll(a.reshape(n // SUBLANES, SUBLANES, w), SUBLANES - k, axis=1)
    return jnp.concatenate([jnp.where(keep[k], tiles[t], tiles[t + 1]) for t in range(n // SUBLANES - 1)],
                           axis=0)


def _ffn_chunks(d_ff, rows):
    step = max(MXU_DIM, (1 << 19) // rows // MXU_DIM * MXU_DIM)
    return tuple((c, min(c + step, d_ff)) for c in range(0, d_ff, step))


def _ffn_in(x, g_ref):
    return _rms(x, g_ref[...]).astype(BF16)


def _ffn_partial_sums(x, h, wg_ref, wu_ref, wd_ref):
    acc = x
    for c0, c1 in _ffn_chunks(wg_ref.shape[1], x.shape[0]):
        gate = _dot(h, wg_ref[:, c0:c1])
        up = _dot(h, wu_ref[:, c0:c1])
        a = (gate * jax.nn.sigmoid(gate) * up).astype(BF16)
        acc = acc + _dot(a, wd_ref[c0:c1, :])
        yield acc


def _ffn(x, h, wg_ref, wu_ref, wd_ref):
    *_, out = _ffn_partial_sums(x, h, wg_ref, wu_ref, wd_ref)
    return out


def _even_mixer_branches(j, xm_ref, xp_ref, xn_ref, g_ref, win_ref, cw_ref, cb_ref, lg_ref, lb_ref,
                         h_s, u_s, p_s, c_s, y_s, pl_s, *, ts, seq, after_block=lambda: None):
    da = u_s.shape[1]
    gp = da // N_HEADS
    rows_in = ts + 2 * HALO
    keep = _row_masks()
    g = g_ref[...]
    h_s[0:HALO, :] = _rms(xp_ref[...], g).astype(BF16)
    h_s[HALO:HALO + ts, :] = _rms(xm_ref[...], g).astype(BF16)
    h_s[HALO + ts:, :] = _rms(xn_ref[...], g).astype(BF16)
    pos = j * ts - HALO + lax.broadcasted_iota(jnp.int32, (rows_in, 1), 0)
    valid = (pos >= 0) & (pos < seq)

    for l0 in range(0, da, LANE_BLOCK):
        lanes = slice(l0, l0 + LANE_BLOCK)
        z = _dot(h_s[...], win_ref[:, 2 * l0:2 * (l0 + LANE_BLOCK)])
        u_s[:, lanes] = jnp.where(valid, z[:, :LANE_BLOCK] * jax.nn.sigmoid(z[:, LANE_BLOCK:]), 0.0)
        for r0 in range(0, ts, ROW_BLOCK):
            out = jnp.broadcast_to(cb_ref[:, lanes], (ROW_BLOCK, LANE_BLOCK))
            for r in range(SUBLANES):
                v = None
                for q in range(pl.cdiv(CONV_WIDTH + HALO - CONV_PAD, SUBLANES)):
                    k = SUBLANES * q + r - (HALO - CONV_PAD)
                    if 0 <= k < CONV_WIDTH:
                        term = (u_s[r0 + SUBLANES * q:r0 + SUBLANES * q + ROW_BLOCK + SUBLANES, lanes]
                                * cw_ref[k:k + 1, lanes])
                        v = term if v is None else v + term
                out = out + _shift_rows(v, r, keep)
            c_s[r0:r0 + ROW_BLOCK, lanes] = out
        after_block()

    p_s[0:rows_in, :] = jnp.where(valid, _dot(h_s[...], win_ref[:, 2 * da:]), 0.0)
    for r0 in range(0, ts, ROW_BLOCK):
        ln = _layernorm(c_s[r0:r0 + ROW_BLOCK, :], lg_ref[...], lb_ref[...])
        y_s[r0:r0 + ROW_BLOCK, 0:da] = (ln * jax.nn.sigmoid(ln)).astype(BF16)
        t = j * ts + r0 + lax.broadcasted_iota(jnp.int32, (ROW_BLOCK, 1), 0)
        for gi, w in enumerate(POOL_WINDOWS):
            half = w // 2
            doublings = half.bit_length() - 1
            lanes = slice(gi * gp, (gi + 1) * gp)
            xw = p_s[r0 + HALO - SUBLANES:r0 + HALO + ROW_BLOCK + SUBLANES * doublings, lanes]
            d = xw
            for i in range(doublings):
                d = d[:-SUBLANES] + _shift_rows(d, 1 << i, keep)
            s = _shift_rows(d, SUBLANES - half, keep) + d[SUBLANES:]
            cnt = (jnp.minimum(t + half - 1, seq - 1) - jnp.maximum(t - half, 0) + 1).astype(F32)
            pooled = s / cnt - xw[SUBLANES:SUBLANES + ROW_BLOCK]
            pl_s[r0:r0 + ROW_BLOCK, lanes] = pooled.astype(BF16)


def _even_back(xm_ref, pw_ref, ps_ref, wout_ref, y_s, pl_s):
    da = pl_s.shape[1]
    y_s[:, da:] = (_dot(pl_s[...], pw_ref[...]) * ps_ref[...]).astype(BF16)
    return xm_ref[...] + _dot(y_s[...], wout_ref[...])


def _layer0_kernel(xm_ref, xp_ref, xn_ref, mg_ref, win_ref, cw_ref, cb_ref, lg_ref, lb_ref,
                   pw_ref, ps_ref, wout_ref, fg_ref, wg_ref, wu_ref, wd_ref, o_ref,
                   h_s, u_s, p_s, c_s, y_s, pl_s, carry_s, hcarry_s, *, ts, seq, n_tiles):
    g = pl.program_id(0)

    @pl.when(g == 0)
    def _():
        carry_s[...] = jnp.zeros_like(carry_s)
        hcarry_s[...] = jnp.zeros_like(hcarry_s)
        p_s[ts + 2 * HALO:, :] = jnp.zeros_like(p_s[ts + 2 * HALO:, :])

    j = jnp.minimum(g, n_tiles - 1) % (seq // ts)
    partial = _ffn_partial_sums(carry_s[...], hcarry_s[...], wg_ref, wu_ref, wd_ref)
    sums = []
    per_block = iter(FFN_CHUNKS_AFTER_BLOCK)

    def ffn_chunk():
        for _ in range(next(per_block, 0)):
            sums.append(next(partial))

    _even_mixer_branches(j, xm_ref, xp_ref, xn_ref, mg_ref, win_ref, cw_ref, cb_ref, lg_ref, lb_ref,
                         h_s, u_s, p_s, c_s, y_s, pl_s, ts=ts, seq=seq, after_block=ffn_chunk)
    sums.extend(partial)
    o_ref[...] = sums[-1]
    x1 = _even_back(xm_ref, pw_ref, ps_ref, wout_ref, y_s, pl_s)
    carry_s[...] = x1
    hcarry_s[...] = _ffn_in(x1, fg_ref)


def _layer0_call(x2d, seq, mix_g, w_in, conv_w, conv_b, ln_g, ln_b, pool_bd, pool_scale, w_out,
                 ffn_g, wg, wu, wd, ts):
    t, d = x2d.shape
    da = conv_w.shape[1]
    nh = ts // HALO
    n_tiles = t // ts
    last_halo = t // HALO - 1
    glu = w_in[:, :2 * da].reshape(d, 2, da // LANE_BLOCK, LANE_BLOCK).transpose(0, 2, 1, 3)
    w_in = jnp.concatenate([glu.reshape(d, 2 * da), w_in[:, 2 * da:]], axis=1)

    def tile(g):
        return jnp.minimum(g, n_tiles - 1)

    kernel = functools.partial(_layer0_kernel, ts=ts, seq=seq, n_tiles=n_tiles)
    return pl.pallas_call(
        kernel,
        grid=(n_tiles + 1,),
        in_specs=[
            pl.BlockSpec((ts, d), lambda g: (tile(g), 0)),
            pl.BlockSpec((HALO, d), lambda g: (jnp.maximum(tile(g) * nh - 1, 0), 0)),
            pl.BlockSpec((HALO, d), lambda g: (jnp.minimum((tile(g) + 1) * nh, last_halo), 0)),
            _const_spec((1, d)), _const_spec(w_in.shape), _const_spec(conv_w.shape),
            _const_spec((1, da)), _const_spec((1, da)), _const_spec((1, da)),
            _const_spec(pool_bd.shape), _const_spec((1, da)), _const_spec(w_out.shape),
            _const_spec((1, d)), _const_spec(wg.shape, 0), _const_spec(wu.shape, 0), _const_spec(wd.shape, 0),
        ],
        out_specs=pl.BlockSpec((ts, d), lambda g: (jnp.maximum(g - 1, 0), 0)),
        out_shape=jax.ShapeDtypeStruct((t, d), F32),
        scratch_shapes=[
            pltpu.VMEM((ts + 2 * HALO, d), BF16),
            pltpu.VMEM((ts + 2 * HALO, da), F32),
            pltpu.VMEM((ts + 2 * HALO + SUBLANES, da), F32),
            pltpu.VMEM((ts, da), F32),
            pltpu.VMEM((ts, d), BF16),
            pltpu.VMEM((ts, da), BF16),
            pltpu.VMEM((ts, d), F32),
            pltpu.VMEM((ts, d), BF16),
        ],
        compiler_params=_params(("arbitrary",)),
        name="layer0",
    )(x2d, x2d, x2d, mix_g.reshape(1, d), w_in, conv_w, conv_b.reshape(1, da), ln_g.reshape(1, da),
      ln_b.reshape(1, da), pool_bd, pool_scale.reshape(1, da), w_out, ffn_g.reshape(1, d), wg, wu, wd)


def _odd_in_kernel(x_ref, g_ref, win_ref, cm_ref, vg_ref, vb_ref, sw_ref, sb_ref,
                   abo_ref, abe_ref, yd_ref, ab_s, *, ts):
    dc = vg_ref.shape[1]
    hd = dc // N_HEADS
    pair = cm_ref.shape[2]
    h = _rms(x_ref[...], g_ref[...]).astype(BF16)
    z = _dot(h, win_ref[...])
    c = z[:, :dc].astype(BF16)
    for q in range(2):
        for pi in range(dc // pair):
            part = _dot(c[:, pi * pair:(pi + 1) * pair], cm_ref[q, pi])
            for sub in range(pair // LANES):
                blk = pi * (pair // LANES) + sub
                lanes = slice(blk * LANES, (blk + 1) * LANES)
                ab_s[blk] = part[:, sub * LANES:(sub + 1) * LANES]
                abo_ref[q, :, lanes] = ab_s[blk, pl.ds(1, ts // 2, stride=2), :].astype(BF16)
                for cls in range(2):
                    abe_ref[q, cls, :, lanes] = ab_s[blk, pl.ds(2 * cls, ts // 4, stride=4), :].astype(BF16)
    zuv = z[:, dc:]
    uv = 0.5 * zuv * (1.0 + lax.erf(zuv * math.sqrt(0.5)))
    u = uv[:, :dc]
    v = uv[:, dc:]
    for hi in range(N_HEADS):
        lanes = slice(hi * hd, (hi + 1) * hd)
        vn = _layernorm(v[:, lanes], vg_ref[:, lanes], vb_ref[:, lanes]).astype(BF16)
        for n in range(ts // CHUNK):
            rows = slice(n * CHUNK, (n + 1) * CHUNK)
            sv = _dot(sw_ref[hi], vn[rows, :]) + sb_ref[:, lanes]
            yd_ref[rows, lanes] = (u[rows, lanes] * sv).astype(BF16)


def _odd_in_call(x, g, w_in, cmix, v_ln_g, v_ln_b, spatial_w, sb_full, ts):
    b, s, d = x.shape
    dc = v_ln_g.shape[0]
    kernel = functools.partial(_odd_in_kernel, ts=ts)
    return pl.pallas_call(
        kernel,
        grid=(b, s // ts),
        in_specs=[
            pl.BlockSpec((None, ts, d), lambda bi, j: (bi, j, 0)),
            _const_spec((1, d)), _const_spec(w_in.shape), _const_spec(cmix.shape),
            _const_spec((1, dc)), _const_spec((1, dc)),
            _const_spec(spatial_w.shape), _const_spec(sb_full.shape),
        ],
        out_specs=[
            pl.BlockSpec((2, ts // 2, dc), lambda bi, j: (0, j, bi)),
            pl.BlockSpec((2, 2, ts // 4, dc), lambda bi, j: (0, 0, j, bi)),
            pl.BlockSpec((None, ts, dc), lambda bi, j: (bi, j, 0)),
        ],
        out_shape=[
            jax.ShapeDtypeStruct((2, s // 2, b * dc), BF16),
            jax.ShapeDtypeStruct((2, 2, s // 4, b * dc), BF16),
            jax.ShapeDtypeStruct((b, s, dc), BF16),
        ],
        scratch_shapes=[pltpu.VMEM((dc // LANES, ts, LANES), F32)],
        compiler_params=_params(("parallel", "parallel")),
        name="odd_mixer_in",
    )(x, g.reshape(1, d), w_in, cmix, v_ln_g.reshape(1, dc), v_ln_b.reshape(1, dc), spatial_w, sb_full)


def _dft_kernel(ce_ref, co_lo_ref, co_hi_ref, abe_ref, abo_ref, o_ref, *, scale):
    def cos_minus_sin(m_ref, a, b):
        return _dot(m_ref[0], a) + _dot(m_ref[1], b)

    ee = cos_minus_sin(ce_ref.at[0], abe_ref[0, 0], abe_ref[1, 0])
    eo = cos_minus_sin(ce_ref.at[1], abe_ref[0, 1], abe_ref[1, 1])
    o_lo = cos_minus_sin(co_lo_ref, abo_ref[0], abo_ref[1])
    o_hi = cos_minus_sin(co_hi_ref, abo_ref[0], abo_ref[1])
    e_lo = ee + eo
    e_hi = ee - eo
    for quarter, val in enumerate((e_lo + o_lo, e_hi + o_hi, e_lo - o_lo, e_hi - o_hi)):
        o_ref[quarter] = (val * scale).astype(o_ref.dtype)


def _dft_call(cm_even, cm_odd, ab_even, ab_odd, scale, tm, tn):
    _, half, nc = ab_odd.shape
    quarter = half // 2
    kernel = functools.partial(_dft_kernel, scale=scale)
    out = pl.pallas_call(
        kernel,
        grid=(quarter // tm, nc // tn),
        in_specs=[
            pl.BlockSpec((2, 2, tm, quarter), lambda i, n: (0, 0, i, 0)),
            pl.BlockSpec((None, 2, tm, half), lambda i, n: (0, 0, i, 0)),
            pl.BlockSpec((None, 2, tm, half), lambda i, n: (0, 0, quarter // tm + i, 0)),
            pl.BlockSpec((2, 2, quarter, tn), lambda i, n: (0, 0, 0, n)),
            pl.BlockSpec((2, half, tn), lambda i, n: (0, 0, n)),
        ],
        out_specs=pl.BlockSpec((4, tm, tn), lambda i, n: (0, i, n)),
        out_shape=jax.ShapeDtypeStruct((4, quarter, nc), BF16),
        compiler_params=_params(("parallel", "parallel")),
        name="seq_dft",
    )(cm_even, cm_odd, cm_odd, ab_even, ab_odd)
    return out.reshape(4 * quarter, nc)


def _layer1_out_kernel(x_ref, yc_ref, yd_ref, wout_ref, fg_ref, wg_ref, wu_ref, wd_ref, ng_ref, o_ref):
    half = x_ref.shape[0] // 2
    x1s = []
    for r0 in (0, half):
        rows = slice(r0, r0 + half)
        y = jnp.concatenate([yc_ref[rows, :], yd_ref[rows, :]], axis=-1)
        x1s.append(x_ref[rows, :] + _dot(y, wout_ref[...]))
    for r0, x1 in zip((0, half), x1s):
        o_ref[r0:r0 + half, :] = _rms(_ffn(x1, _ffn_in(x1, fg_ref), wg_ref, wu_ref, wd_ref), ng_ref[...])


def _layer1_out_call(x, yc, yd, w_out, ffn_g, wg, wu, wd, final_g, ts):
    b, s, d = x.shape
    dc = yd.shape[2]
    return pl.pallas_call(
        _layer1_out_kernel,
        grid=(b, s // ts),
        in_specs=[
            pl.BlockSpec((None, ts, d), lambda bi, j: (bi, j, 0)),
            pl.BlockSpec((ts, dc), lambda bi, j: (j, bi)),
            pl.BlockSpec((None, ts, dc), lambda bi, j: (bi, j, 0)),
            _const_spec(w_out.shape),
            _const_spec((1, d)), _const_spec(wg.shape, 1), _const_spec(wu.shape, 1), _const_spec(wd.shape, 1),
            _const_spec((1, d)),
        ],
        out_specs=pl.BlockSpec((None, ts, d), lambda bi, j: (bi, j, 0)),
        out_shape=jax.ShapeDtypeStruct((b, s, d), F32),
        compiler_params=_params(("parallel", "parallel")),
        name="layer1_out",
    )(x, yc, yd, w_out, ffn_g.reshape(1, d), wg, wu, wd, final_g.reshape(1, d))


def _block_diag(w):
    g, n, _ = w.shape
    eye = jnp.eye(g, dtype=w.dtype)
    return (eye[:, None, :, None] * w[:, :, None, :]).reshape(g * n, g * n)


def _channel_dft_mats(n):
    idx = np.outer(np.arange(n), np.arange(n)) % n
    ang = 2.0 * np.pi * idx / n
    return np.stack([np.cos(ang), np.sin(ang)]).astype(np.float32)


def _fold_kernel(t_ref, w_ref, o_ref):
    n = t_ref.shape[1]
    o_ref[...] = jnp.zeros_like(o_ref)
    for q in range(2):
        for pi in range(o_ref.shape[1]):
            for sub in range(o_ref.shape[2] // n):
                blk = slice(sub * n, (sub + 1) * n)
                head = pi * (o_ref.shape[2] // n) + sub
                o_ref[q, pi, blk, blk] = jnp.dot(t_ref[q], w_ref[head], preferred_element_type=F32,
                                                 precision=lax.Precision.HIGHEST).astype(o_ref.dtype)


def _fold_channel_mix(fourier_w):
    heads, n, _ = fourier_w.shape
    per = MXU_DIM // n
    return pl.pallas_call(
        _fold_kernel,
        out_shape=jax.ShapeDtypeStruct((2, heads // per, per * n, per * n), BF16),
        name="fold_channel_mix",
    )(jnp.asarray(_channel_dft_mats(n)), fourier_w)


def _twiddle_kernel(ta_ref, tb_ref, o_ref):
    for blk in range(o_ref.shape[3] // LANES):
        ca = ta_ref[0, :, blk:blk + 1]
        sa = ta_ref[1, :, blk:blk + 1]
        cols = slice(blk * LANES, (blk + 1) * LANES)
        for pi in range(o_ref.shape[0]):
            cb = tb_ref[pi, 0]
            sb = tb_ref[pi, 1]
            o_ref[pi, 0, :, cols] = (ca * cb - sa * sb).astype(o_ref.dtype)
            o_ref[pi, 1, :, cols] = (-(sa * cb) - ca * sb).astype(o_ref.dtype)


def _seq_dft_mats(s, rows, parities):
    n_par = len(parities)
    half = s // 2
    nblk = half // LANES
    unit = 2.0 * math.pi / s

    def cos_sin(idx):
        ang = (idx % s).astype(F32) * unit
        return jnp.stack([jnp.cos(ang), jnp.sin(ang)])

    j = lax.broadcasted_iota(jnp.int32, (half, nblk), 0)
    blk = lax.broadcasted_iota(jnp.int32, (half, nblk), 1)
    ta = cos_sin(j * (2 * LANES * blk))
    j = lax.broadcasted_iota(jnp.int32, (half, LANES), 0)
    ml = lax.broadcasted_iota(jnp.int32, (half, LANES), 1)
    tb = jnp.stack([cos_sin(j * (2 * ml + parity)) for parity in parities])
    return pl.pallas_call(
        _twiddle_kernel,
        grid=(half // rows,),
        in_specs=[pl.BlockSpec((2, rows, nblk), lambda i: (0, i, 0)),
                  pl.BlockSpec((n_par, 2, rows, LANES), lambda i: (0, 0, i, 0))],
        out_specs=pl.BlockSpec((n_par, 2, rows, half), lambda i: (0, 0, i, 0)),
        out_shape=jax.ShapeDtypeStruct((n_par, 2, half, half), BF16),
        compiler_params=_params(("parallel",)),
        name=f"seq_dft_twiddles_{s}",
    )(ta, tb)


def kernel(x, mix_norm_g, ffn_norm_g, ev_w_in, ev_conv_w, ev_conv_b, ev_ln_g, ev_ln_b, ev_pool_w,
           ev_pool_scale, ev_w_out, od_w_in, od_fourier_w, od_v_ln_g, od_v_ln_b, od_spatial_w,
           od_spatial_b, od_w_out, ffn_w_gate, ffn_w_up, ffn_w_down, final_norm_g):
    b, s, d = x.shape
    assert mix_norm_g.shape[0] == 2, "one conv/pool layer followed by one fourier/gating layer"
    hc = od_fourier_w.shape[2]
    ts = 512
    ts1 = 1024

    cm_odd = _seq_dft_mats(s, 256, (1,))
    cm_even = _seq_dft_mats(s // 2, 256, (0, 1))
    dft_scale = 1.0 / math.sqrt(s * hc)
    wg, wu, wd = (w.astype(BF16) for w in (ffn_w_gate, ffn_w_up, ffn_w_down))

    x = _layer0_call(x.reshape(b * s, d), s, mix_norm_g[0], ev_w_in[0].astype(BF16), ev_conv_w[0],
                     ev_conv_b[0], ev_ln_g[0], ev_ln_b[0], _block_diag(ev_pool_w[0]).astype(BF16),
                     ev_pool_scale[0].reshape(-1), ev_w_out[0].astype(BF16),
                     ffn_norm_g[0], wg, wu, wd, ts).reshape(b, s, d)

    sb_full = jnp.broadcast_to(od_spatial_b[0].T[:, :, None], (CHUNK, N_HEADS, hc))
    ab_odd, ab_even, yd = _odd_in_call(
        x, mix_norm_g[1], od_w_in[0].astype(BF16), _fold_channel_mix(od_fourier_w[0]),
        od_v_ln_g[0].reshape(-1), od_v_ln_b[0].reshape(-1), od_spatial_w[0].astype(BF16),
        sb_full.reshape(CHUNK, N_HEADS * hc), ts1)
    yc = _dft_call(cm_even, cm_odd, ab_even, ab_odd, dft_scale, 512, 512)
    return _layer1_out_call(x, yc, yd, od_w_out[0].astype(BF16), ffn_norm_g[1], wg, wu, wd,
                            final_norm_g, ts1)
```

```python
import functools
import math

import numpy as np
import jax
import jax.numpy as jnp
from jax import lax
from jax.experimental import pallas as pl
from jax.experimental.pallas import tpu as pltpu

EPS = 1e-6
CONV_WIDTH = 31
CONV_PAD = CONV_WIDTH // 2
POOL_WINDOWS = (2, 4, 8, 16)
N_HEADS = 4
CHUNK = 128
HALO = 16
ROW_BLOCK = 64
LANE_BLOCK = 128
FFN_CHUNKS_AFTER_BLOCK = (0, 0, 2, 0)
SUBLANES = 8
LANES = 128
MXU_DIM = 256
VMEM_LIMIT = 56 * 1024 * 1024

F32 = jnp.float32
BF16 = jnp.bfloat16


def _rms(xv, g):
    ms = jnp.mean(xv * xv, axis=-1, keepdims=True)
    return xv * lax.rsqrt(ms + EPS) * g


def _layernorm(xv, g, b):
    mu = jnp.mean(xv, axis=-1, keepdims=True)
    d = xv - mu
    var = jnp.mean(d * d, axis=-1, keepdims=True)
    return d * lax.rsqrt(var + EPS) * g + b


def _dot(a, b):
    return jnp.dot(a, b, preferred_element_type=F32)


def _const_spec(shape, layer=None):
    if layer is None:
        return pl.BlockSpec(shape, lambda *_: (0,) * len(shape), pipeline_mode=pl.Buffered(1))
    return pl.BlockSpec((None,) + tuple(shape[1:]), lambda *_: (layer,) + (0,) * (len(shape) - 1),
                        pipeline_mode=pl.Buffered(1))


def _params(semantics):
    return pltpu.CompilerParams(dimension_semantics=semantics, vmem_limit_bytes=VMEM_LIMIT)


def _row_masks():
    sub = lax.broadcasted_iota(jnp.int32, (SUBLANES, LANES), 0)
    return {k: sub < SUBLANES - k for k in range(1, SUBLANES)}


def _shift_rows(a, k, keep):
    if k == 0:
        return a[:-SUBLANES]
    n, w = a.shape
    tiles = pltpu.roll(a.reshape(n // SUBLANES, SUBLANES, w), SUBLANES - k, axis=1)
    return jnp.concatenate([jnp.where(keep[k], tiles[t], tiles[t + 1]) for t in range(n // SUBLANES - 1)],
                           axis=0)


def _ffn_chunks(d_ff, rows):
    step = max(MXU_DIM, (1 << 19) // rows // MXU_DIM * MXU_DIM)
    return tuple((c, min(c + step, d_ff)) for c in range(0, d_ff, step))


def _ffn_in(x, g_ref):
    return _rms(x, g_ref[...]).astype(BF16)


def _ffn_partial_sums(x, h, wg_ref, wu_ref, wd_ref):
    acc = x
    for c0, c1 in _ffn_chunks(wg_ref.shape[1], x.shape[0]):
        gate = _dot(h, wg_ref[:, c0:c1])
        up = _dot(h, wu_ref[:, c0:c1])
        a = (gate * jax.nn.sigmoid(gate) * up).astype(BF16)
        acc = acc + _dot(a, wd_ref[c0:c1, :])
        yield acc


def _ffn(x, h, wg_ref, wu_ref, wd_ref):
    *_, out = _ffn_partial_sums(x, h, wg_ref, wu_ref, wd_ref)
    return out


def _even_mixer_branches(j, xm_ref, xp_ref, xn_ref, g_ref, win_ref, cw_ref, cb_ref, lg_ref, lb_ref,
                         h_s, u_s, p_s, c_s, y_s, pl_s, *, ts, seq, after_block=lambda: None):
    da = u_s.shape[1]
    gp = da // N_HEADS
    rows_in = ts + 2 * HALO
    keep = _row_masks()
    g = g_ref[...]
    h_s[0:HALO, :] = _rms(xp_ref[...], g).astype(BF16)
    h_s[HALO:HALO + ts, :] = _rms(xm_ref[...], g).astype(BF16)
    h_s[HALO + ts:, :] = _rms(xn_ref[...], g).astype(BF16)
    pos = j * ts - HALO + lax.broadcasted_iota(jnp.int32, (rows_in, 1), 0)
    valid = (pos >= 0) & (pos < seq)

    for l0 in range(0, da, LANE_BLOCK):
        lanes = slice(l0, l0 + LANE_BLOCK)
        z = _dot(h_s[...], win_ref[:, 2 * l0:2 * (l0 + LANE_BLOCK)])
        u_s[:, lanes] = jnp.where(valid, z[:, :LANE_BLOCK] * jax.nn.sigmoid(z[:, LANE_BLOCK:]), 0.0)
        for r0 in range(0, ts, ROW_BLOCK):
            out = jnp.broadcast_to(cb_ref[:, lanes], (ROW_BLOCK, LANE_BLOCK))
            for r in range(SUBLANES):
                v = None
                for q in range(pl.cdiv(CONV_WIDTH + HALO - CONV_PAD, SUBLANES)):
                    k = SUBLANES * q + r - (HALO - CONV_PAD)
                    if 0 <= k < CONV_WIDTH:
                        term = (u_s[r0 + SUBLANES * q:r0 + SUBLANES * q + ROW_BLOCK + SUBLANES, lanes]
                                * cw_ref[k:k + 1, lanes])
                        v = term if v is None else v + term
                out = out + _shift_rows(v, r, keep)
            c_s[r0:r0 + ROW_BLOCK, lanes] = out
        after_block()

    p_s[0:rows_in, :] = jnp.where(valid, _dot(h_s[...], win_ref[:, 2 * da:]), 0.0)
    for r0 in range(0, ts, ROW_BLOCK):
        ln = _layernorm(c_s[r0:r0 + ROW_BLOCK, :], lg_ref[...], lb_ref[...])
        y_s[r0:r0 + ROW_BLOCK, 0:da] = (ln * jax.nn.sigmoid(ln)).astype(BF16)
        t = j * ts + r0 + lax.broadcasted_iota(jnp.int32, (ROW_BLOCK, 1), 0)
        for gi, w in enumerate(POOL_WINDOWS):
            half = w // 2
            doublings = half.bit_length() - 1
            lanes = slice(gi * gp, (gi + 1) * gp)
            xw = p_s[r0 + HALO - SUBLANES:r0 + HALO + ROW_BLOCK + SUBLANES * doublings, lanes]
            d = xw
            for i in range(doublings):
                d = d[:-SUBLANES] + _shift_rows(d, 1 << i, keep)
            s = _shift_rows(d, SUBLANES - half, keep) + d[SUBLANES:]
            cnt = (jnp.minimum(t + half - 1, seq - 1) - jnp.maximum(t - half, 0) + 1).astype(F32)
            pooled = s / cnt - xw[SUBLANES:SUBLANES + ROW_BLOCK]
            pl_s[r0:r0 + ROW_BLOCK, lanes] = pooled.astype(BF16)


def _even_back(xm_ref, pw_ref, ps_ref, wout_ref, y_s, pl_s):
    da = pl_s.shape[1]
    y_s[:, da:] = (_dot(pl_s[...], pw_ref[...]) * ps_ref[...]).astype(BF16)
    return xm_ref[...] + _dot(y_s[...], wout_ref[...])


def _layer0_kernel(xm_ref, xp_ref, xn_ref, mg_ref, win_ref, cw_ref, cb_ref, lg_ref, lb_ref,
                   pw_ref, ps_ref, wout_ref, fg_ref, wg_ref, wu_ref, wd_ref, o_ref,
                   h_s, u_s, p_s, c_s, y_s, pl_s, carry_s, hcarry_s, *, ts, seq, n_tiles):
    g = pl.program_id(0)

    @pl.when(g == 0)
    def _():
        carry_s[...] = jnp.zeros_like(carry_s)
        hcarry_s[...] = jnp.zeros_like(hcarry_s)
        p_s[ts + 2 * HALO:, :] = jnp.zeros_like(p_s[ts + 2 * HALO:, :])

    j = jnp.minimum(g, n_tiles - 1) % (seq // ts)
    partial = _ffn_partial_sums(carry_s[...], hcarry_s[...], wg_ref, wu_ref, wd_ref)
    sums = []
    per_block = iter(FFN_CHUNKS_AFTER_BLOCK)

    def ffn_chunk():
        for _ in range(next(per_block, 0)):
            sums.append(next(partial))

    _even_mixer_branches(j, xm_ref, xp_ref, xn_ref, mg_ref, win_ref, cw_ref, cb_ref, lg_ref, lb_ref,
                         h_s, u_s, p_s, c_s, y_s, pl_s, ts=ts, seq=seq, after_block=ffn_chunk)
    sums.extend(partial)
    o_ref[...] = sums[-1]
    x1 = _even_back(xm_ref, pw_ref, ps_ref, wout_ref, y_s, pl_s)
    carry_s[...] = x1
    hcarry_s[...] = _ffn_in(x1, fg_ref)


def _layer0_call(x2d, seq, mix_g, w_in, conv_w, conv_b, ln_g, ln_b, pool_bd, pool_scale, w_out,
                 ffn_g, wg, wu, wd, ts):
    t, d = x2d.shape
    da = conv_w.shape[1]
    nh = ts // HALO
    n_tiles = t // ts
    last_halo = t // HALO - 1
    glu = w_in[:, :2 * da].reshape(d, 2, da // LANE_BLOCK, LANE_BLOCK).transpose(0, 2, 1, 3)
    w_in = jnp.concatenate([glu.reshape(d, 2 * da), w_in[:, 2 * da:]], axis=1)

    def tile(g):
        return jnp.minimum(g, n_tiles - 1)

    kernel = functools.partial(_layer0_kernel, ts=ts, seq=seq, n_tiles=n_tiles)
    return pl.pallas_call(
        kernel,
        grid=(n_tiles + 1,),
        in_specs=[
            pl.BlockSpec((ts, d), lambda g: (tile(g), 0)),
            pl.BlockSpec((HALO, d), lambda g: (jnp.maximum(tile(g) * nh - 1, 0), 0)),
            pl.BlockSpec((HALO, d), lambda g: (jnp.minimum((tile(g) + 1) * nh, last_halo), 0)),
            _const_spec((1, d)), _const_spec(w_in.shape), _const_spec(conv_w.shape),
            _const_spec((1, da)), _const_spec((1, da)), _const_spec((1, da)),
            _const_spec(pool_bd.shape), _const_spec((1, da)), _const_spec(w_out.shape),
            _const_spec((1, d)), _const_spec(wg.shape, 0), _const_spec(wu.shape, 0), _const_spec(wd.shape, 0),
        ],
        out_specs=pl.BlockSpec((ts, d), lambda g: (jnp.maximum(g - 1, 0), 0)),
        out_shape=jax.ShapeDtypeStruct((t, d), F32),
        scratch_shapes=[
            pltpu.VMEM((ts + 2 * HALO, d), BF16),
            pltpu.VMEM((ts + 2 * HALO, da), F32),
            pltpu.VMEM((ts + 2 * HALO + SUBLANES, da), F32),
            pltpu.VMEM((ts, da), F32),
            pltpu.VMEM((ts, d), BF16),
            pltpu.VMEM((ts, da), BF16),
            pltpu.VMEM((ts, d), F32),
            pltpu.VMEM((ts, d), BF16),
        ],
        compiler_params=_params(("arbitrary",)),
        name="layer0",
    )(x2d, x2d, x2d, mix_g.reshape(1, d), w_in, conv_w, conv_b.reshape(1, da), ln_g.reshape(1, da),
      ln_b.reshape(1, da), pool_bd, pool_scale.reshape(1, da), w_out, ffn_g.reshape(1, d), wg, wu, wd)


def _odd_in_kernel(x_ref, g_ref, win_ref, cm_ref, vg_ref, vb_ref, sw_ref, sb_ref,
                   abo_ref, abe_ref, yd_ref, ab_s, *, ts):
    dc = vg_ref.shape[1]
    hd = dc // N_HEADS
    pair = cm_ref.shape[2]
    h = _rms(x_ref[...], g_ref[...]).astype(BF16)

    def gelu(zz):
        return 0.5 * zz * (1.0 + lax.erf(zz * math.sqrt(0.5)))

    v = gelu(_dot(h, win_ref[:, 2 * dc:]))
    c = _dot(h, win_ref[:, :dc]).astype(BF16)
    for q in range(2):
        for pi in range(dc // pair):
            part = _dot(c[:, pi * pair:(pi + 1) * pair], cm_ref[q, pi])
            for sub in range(pair // LANES):
                blk = pi * (pair // LANES) + sub
                lanes = slice(blk * LANES, (blk + 1) * LANES)
                ab_s[blk] = part[:, sub * LANES:(sub + 1) * LANES]
                abo_ref[q, :, lanes] = ab_s[blk, pl.ds(1, ts // 2, stride=2), :].astype(BF16)
                for cls in range(2):
                    abe_ref[q, cls, :, lanes] = ab_s[blk, pl.ds(2 * cls, ts // 4, stride=4), :].astype(BF16)
    u = gelu(_dot(h, win_ref[:, dc:2 * dc]))
    for hi in range(N_HEADS):
        lanes = slice(hi * hd, (hi + 1) * hd)
        vn = _layernorm(v[:, lanes], vg_ref[:, lanes], vb_ref[:, lanes]).astype(BF16)
        for n in range(ts // CHUNK):
            rows = slice(n * CHUNK, (n + 1) * CHUNK)
            sv = _dot(sw_ref[hi], vn[rows, :]) + sb_ref[:, lanes]
            yd_ref[rows, lanes] = (u[rows, lanes] * sv).astype(BF16)


def _odd_in_call(x, g, w_in, cmix, v_ln_g, v_ln_b, spatial_w, sb_full, ts):
    b, s, d = x.shape
    dc = v_ln_g.shape[0]
    kernel = functools.partial(_odd_in_kernel, ts=ts)
    return pl.pallas_call(
        kernel,
        grid=(b, s // ts),
        in_specs=[
            pl.BlockSpec((None, ts, d), lambda bi, j: (bi, j, 0)),
            _const_spec((1, d)), _const_spec(w_in.shape), _const_spec(cmix.shape),
            _const_spec((1, dc)), _const_spec((1, dc)),
            _const_spec(spatial_w.shape), _const_spec(sb_full.shape),
        ],
        out_specs=[
            pl.BlockSpec((2, ts // 2, dc), lambda bi, j: (0, j, bi)),
            pl.BlockSpec((2, 2, ts // 4, dc), lambda bi, j: (0, 0, j, bi)),
            pl.BlockSpec((None, ts, dc), lambda bi, j: (bi, j, 0)),
        ],
        out_shape=[
            jax.ShapeDtypeStruct((2, s // 2, b * dc), BF16),
            jax.ShapeDtypeStruct((2, 2, s // 4, b * dc), BF16),
            jax.ShapeDtypeStruct((b, s, dc), BF16),
        ],
        scratch_shapes=[pltpu.VMEM((dc // LANES, ts, LANES), F32)],
        compiler_params=_params(("parallel", "parallel")),
        name="odd_mixer_in",
    )(x, g.reshape(1, d), w_in, cmix, v_ln_g.reshape(1, dc), v_ln_b.reshape(1, dc), spatial_w, sb_full)


def _dft_kernel(ce_ref, co_lo_ref, co_hi_ref, abe_ref, abo_ref, o_ref, *, scale):
    def cos_minus_sin(m_ref, a, b):
        return _dot(m_ref[0], a) + _dot(m_ref[1], b)

    ee = cos_minus_sin(ce_ref.at[0], abe_ref[0, 0], abe_ref[1, 0])
    eo = cos_minus_sin(ce_ref.at[1], abe_ref[0, 1], abe_ref[1, 1])
    o_lo = cos_minus_sin(co_lo_ref, abo_ref[0], abo_ref[1])
    o_hi = cos_minus_sin(co_hi_ref, abo_ref[0], abo_ref[1])
    e_lo = ee + eo
    e_hi = ee - eo
    for quarter, val in enumerate((e_lo + o_lo, e_hi + o_hi, e_lo - o_lo, e_hi - o_hi)):
        o_ref[quarter] = (val * scale).astype(o_ref.dtype)


def _dft_call(cm_even, cm_odd, ab_even, ab_odd, scale, tm, tn):
    _, half, nc = ab_odd.shape
    quarter = half // 2
    kernel = functools.partial(_dft_kernel, scale=scale)
    out = pl.pallas_call(
        kernel,
        grid=(quarter // tm, nc // tn),
        in_specs=[
            pl.BlockSpec((2, 2, tm, quarter), lambda i, n: (0, 0, i, 0)),
            pl.BlockSpec((None, 2, tm, half), lambda i, n: (0, 0, i, 0)),
            pl.BlockSpec((None, 2, tm, half), lambda i, n: (0, 0, quarter // tm + i, 0)),
            pl.BlockSpec((2, 2, quarter, tn), lambda i, n: (0, 0, 0, n)),
            pl.BlockSpec((2, half, tn), lambda i, n: (0, 0, n)),
        ],
        out_specs=pl.BlockSpec((4, tm, tn), lambda i, n: (0, i, n)),
        out_shape=jax.ShapeDtypeStruct((4, quarter, nc), BF16),
        compiler_params=_params(("parallel", "parallel")),
        name="seq_dft",
    )(cm_even, cm_odd, cm_odd, ab_even, ab_odd)
    return out.reshape(4 * quarter, nc)


def _layer1_out_kernel(x_ref, yc_ref, yd_ref, wout_ref, fg_ref, wg_ref, wu_ref, wd_ref, ng_ref, o_ref):
    half = x_ref.shape[0] // 2
    x1s = []
    for r0 in (0, half):
        rows = slice(r0, r0 + half)
        y = jnp.concatenate([yc_ref[rows, :], yd_ref[rows, :]], axis=-1)
        x1s.append(x_ref[rows, :] + _dot(y, wout_ref[...]))
    for r0, x1 in zip((0, half), x1s):
        o_ref[r0:r0 + half, :] = _rms(_ffn(x1, _ffn_in(x1, fg_ref), wg_ref, wu_ref, wd_ref), ng_ref[...])


def _layer1_out_call(x, yc, yd, w_out, ffn_g, wg, wu, wd, final_g, ts):
    b, s, d = x.shape
    dc = yd.shape[2]
    return pl.pallas_call(
        _layer1_out_kernel,
        grid=(b, s // ts),
        in_specs=[
            pl.BlockSpec((None, ts, d), lambda bi, j: (bi, j, 0)),
            pl.BlockSpec((ts, dc), lambda bi, j: (j, bi)),
            pl.BlockSpec((None, ts, dc), lambda bi, j: (bi, j, 0)),
            _const_spec(w_out.shape),
            _const_spec((1, d)), _const_spec(wg.shape, 1), _const_spec(wu.shape, 1), _const_spec(wd.shape, 1),
            _const_spec((1, d)),
        ],
        out_specs=pl.BlockSpec((None, ts, d), lambda bi, j: (bi, j, 0)),
        out_shape=jax.ShapeDtypeStruct((b, s, d), F32),
        compiler_params=_params(("parallel", "parallel")),
        name="layer1_out",
    )(x, yc, yd, w_out, ffn_g.reshape(1, d), wg, wu, wd, final_g.reshape(1, d))


def _block_diag(w):
    g, n, _ = w.shape
    eye = jnp.eye(g, dtype=w.dtype)
    return (eye[:, None, :, None] * w[:, :, None, :]).reshape(g * n, g * n)


def _channel_dft_mats(n):
    idx = np.outer(np.arange(n), np.arange(n)) % n
    ang = 2.0 * np.pi * idx / n
    return np.stack([np.cos(ang), np.sin(ang)]).astype(np.float32)


def _fold_kernel(t_ref, w_ref, o_ref):
    n = t_ref.shape[1]
    o_ref[...] = jnp.zeros_like(o_ref)
    for q in range(2):
        for pi in range(o_ref.shape[1]):
            for sub in range(o_ref.shape[2] // n):
                blk = slice(sub * n, (sub + 1) * n)
                head = pi * (o_ref.shape[2] // n) + sub
                o_ref[q, pi, blk, blk] = jnp.dot(t_ref[q], w_ref[head], preferred_element_type=F32,
                                                 precision=lax.Precision.HIGHEST).astype(o_ref.dtype)


def _fold_channel_mix(fourier_w):
    heads, n, _ = fourier_w.shape
    per = MXU_DIM // n
    return pl.pallas_call(
        _fold_kernel,
        out_shape=jax.ShapeDtypeStruct((2, heads // per, per * n, per * n), BF16),
        name="fold_channel_mix",
    )(jnp.asarray(_channel_dft_mats(n)), fourier_w)


def _twiddle_kernel(ta_ref, tb_ref, o_ref):
    for blk in range(o_ref.shape[3] // LANES):
        ca = ta_ref[0, :, blk:blk + 1]
        sa = ta_ref[1, :, blk:blk + 1]
        cols = slice(blk * LANES, (blk + 1) * LANES)
        for pi in range(o_ref.shape[0]):
            cb = tb_ref[pi, 0]
            sb = tb_ref[pi, 1]
            o_ref[pi, 0, :, cols] = (ca * cb - sa * sb).astype(o_ref.dtype)
            o_ref[pi, 1, :, cols] = (-(sa * cb) - ca * sb).astype(o_ref.dtype)


def _seq_dft_mats(s, rows, parities):
    n_par = len(parities)
    half = s // 2
    nblk = half // LANES
    unit = 2.0 * math.pi / s

    def cos_sin(idx):
        ang = (idx % s).astype(F32) * unit
        return jnp.stack([jnp.cos(ang), jnp.sin(ang)])

    j = lax.broadcasted_iota(jnp.int32, (half, nblk), 0)
    blk = lax.broadcasted_iota(jnp.int32, (half, nblk), 1)
    ta = cos_sin(j * (2 * LANES * blk))
    j = lax.broadcasted_iota(jnp.int32, (half, LANES), 0)
    ml = lax.broadcasted_iota(jnp.int32, (half, LANES), 1)
    tb = jnp.stack([cos_sin(j * (2 * ml + parity)) for parity in parities])
    return pl.pallas_call(
        _twiddle_kernel,
        grid=(half // rows,),
        in_specs=[pl.BlockSpec((2, rows, nblk), lambda i: (0, i, 0)),
                  pl.BlockSpec((n_par, 2, rows, LANES), lambda i: (0, 0, i, 0))],
        out_specs=pl.BlockSpec((n_par, 2, rows, half), lambda i: (0, 0, i, 0)),
        out_shape=jax.ShapeDtypeStruct((n_par, 2, half, half), BF16),
        compiler_params=_params(("parallel",)),
        name=f"seq_dft_twiddles_{s}",
    )(ta, tb)


def kernel(x, mix_norm_g, ffn_norm_g, ev_w_in, ev_conv_w, ev_conv_b, ev_ln_g, ev_ln_b, ev_pool_w,
           ev_pool_scale, ev_w_out, od_w_in, od_fourier_w, od_v_ln_g, od_v_ln_b, od_spatial_w,
           od_spatial_b, od_w_out, ffn_w_gate, ffn_w_up, ffn_w_down, final_norm_g):
    b, s, d = x.shape
    assert mix_norm_g.shape[0] == 2, "one conv/pool layer followed by one fourier/gating layer"
    hc = od_fourier_w.shape[2]
    ts = 512
    ts1 = 1024

    cm_odd = _seq_dft_mats(s, 256, (1,))
    cm_even = _seq_dft_mats(s // 2, 256, (0, 1))
    dft_scale = 1.0 / math.sqrt(s * hc)
    wg, wu, wd = (w.astype(BF16) for w in (ffn_w_gate, ffn_w_up, ffn_w_down))

    x = _layer0_call(x.reshape(b * s, d), s, mix_norm_g[0], ev_w_in[0].astype(BF16), ev_conv_w[0],
                     ev_conv_b[0], ev_ln_g[0], ev_ln_b[0], _block_diag(ev_pool_w[0]).astype(BF16),
                     ev_pool_scale[0].reshape(-1), ev_w_out[0].astype(BF16),
                     ffn_norm_g[0], wg, wu, wd, ts).reshape(b, s, d)

    sb_full = jnp.broadcast_to(od_spatial_b[0].T[:, :, None], (CHUNK, N_HEADS, hc))
    ab_odd, ab_even, yd = _odd_in_call(
        x, mix_norm_g[1], od_w_in[0].astype(BF16), _fold_channel_mix(od_fourier_w[0]),
        od_v_ln_g[0].reshape(-1), od_v_ln_b[0].reshape(-1), od_spatial_w[0].astype(BF16),
        sb_full.reshape(CHUNK, N_HEADS * hc), ts1)
    yc = _dft_call(cm_even, cm_odd, ab_even, ab_odd, dft_scale, 512, 512)
    return _layer1_out_call(x, yc, yd, od_w_out[0].astype(BF16), ffn_norm_g[1], wg, wu, wd,
                            final_norm_g, ts1)
```

```python
import functools
import math

import numpy as np
import jax
import jax.numpy as jnp
from jax import lax
from jax.experimental import pallas as pl
from jax.experimental.pallas import tpu as pltpu

EPS = 1e-6
CONV_WIDTH = 31
CONV_PAD = CONV_WIDTH // 2
POOL_WINDOWS = (2, 4, 8, 16)
N_HEADS = 4
CHUNK = 128
HALO = 16
ROW_BLOCK = 64
LANE_BLOCK = 128
FFN_CHUNKS_AFTER_BLOCK = (0, 0, 2, 0)
SUBLANES = 8
LANES = 128
MXU_DIM = 256
VMEM_LIMIT = 56 * 1024 * 1024

F32 = jnp.float32
BF16 = jnp.bfloat16


def _rms(xv, g):
    ms = jnp.mean(xv * xv, axis=-1, keepdims=True)
    return xv * lax.rsqrt(ms + EPS) * g


def _layernorm(xv, g, b):
    mu = jnp.mean(xv, axis=-1, keepdims=True)
    d = xv - mu
    var = jnp.mean(d * d, axis=-1, keepdims=True)
    return d * lax.rsqrt(var + EPS) * g + b


def _dot(a, b):
    return jnp.dot(a, b, preferred_element_type=F32)


def _const_spec(shape, layer=None):
    if layer is None:
        return pl.BlockSpec(shape, lambda *_: (0,) * len(shape), pipeline_mode=pl.Buffered(1))
    return pl.BlockSpec((None,) + tuple(shape[1:]), lambda *_: (layer,) + (0,) * (len(shape) - 1),
                        pipeline_mode=pl.Buffered(1))


def _params(semantics):
    return pltpu.CompilerParams(dimension_semantics=semantics, vmem_limit_bytes=VMEM_LIMIT)


def _row_masks():
    sub = lax.broadcasted_iota(jnp.int32, (SUBLANES, LANES), 0)
    return {k: sub < SUBLANES - k for k in range(1, SUBLANES)}


def _shift_rows(a, k, keep):
    if k == 0:
        return a[:-SUBLANES]
    n, w = a.shape
    tiles = pltpu.roll(a.reshape(n // SUBLANES, SUBLANES, w), SUBLANES - k, axis=1)
    return jnp.concatenate([jnp.where(keep[k], tiles[t], tiles[t + 1]) for t in range(n // SUBLANES - 1)],
                           axis=0)


def _ffn_chunks(d_ff, rows):
    step = max(MXU_DIM, (1 << 19) // rows // MXU_DIM * MXU_DIM)
    return tuple((c, min(c + step, d_ff)) for c in range(0, d_ff, step))


def _ffn_in(x, g_ref):
    return _rms(x, g_ref[...]).astype(BF16)


def _ffn_partial_sums(x, h, wg_ref, wu_ref, wd_ref):
    acc = x
    for c0, c1 in _ffn_chunks(wg_ref.shape[1], x.shape[0]):
        gate = _dot(h, wg_ref[:, c0:c1])
        up = _dot(h, wu_ref[:, c0:c1])
        a = (gate * jax.nn.sigmoid(gate) * up).astype(BF16)
        acc = acc + _dot(a, wd_ref[c0:c1, :])
        yield acc


def _ffn(x, h, wg_ref, wu_ref, wd_ref):
    *_, out = _ffn_partial_sums(x, h, wg_ref, wu_ref, wd_ref)
    return out


def _even_mixer_branches(j, xm_ref, xp_ref, xn_ref, g_ref, win_ref, cw_ref, cb_ref, lg_ref, lb_ref,
                         h_s, u_s, p_s, c_s, y_s, pl_s, *, ts, seq, after_block=lambda: None):
    da = u_s.shape[1]
    gp = da // N_HEADS
    rows_in = ts + 2 * HALO
    keep = _row_masks()
    g = g_ref[...]
    h_s[0:HALO, :] = _rms(xp_ref[...], g).astype(BF16)
    h_s[HALO:HALO + ts, :] = _rms(xm_ref[...], g).astype(BF16)
    h_s[HALO + ts:, :] = _rms(xn_ref[...], g).astype(BF16)
    pos = j * ts - HALO + lax.broadcasted_iota(jnp.int32, (rows_in, 1), 0)
    valid = (pos >= 0) & (pos < seq)

    for l0 in range(0, da, LANE_BLOCK):
        lanes = slice(l0, l0 + LANE_BLOCK)
        z = _dot(h_s[...], win_ref[:, 2 * l0:2 * (l0 + LANE_BLOCK)])
        u_s[:, lanes] = jnp.where(valid, z[:, :LANE_BLOCK] * jax.nn.sigmoid(z[:, LANE_BLOCK:]), 0.0)
        for r0 in range(0, ts, ROW_BLOCK):
            out = jnp.broadcast_to(cb_ref[:, lanes], (ROW_BLOCK, LANE_BLOCK))
            for r in range(SUBLANES):
                v = None
                for q in range(pl.cdiv(CONV_WIDTH + HALO - CONV_PAD, SUBLANES)):
                    k = SUBLANES * q + r - (HALO - CONV_PAD)
                    if 0 <= k < CONV_WIDTH:
                        term = (u_s[r0 + SUBLANES * q:r0 + SUBLANES * q + ROW_BLOCK + SUBLANES, lanes]
                                * cw_ref[k:k + 1, lanes])
                        v = term if v is None else v + term
                out = out + _shift_rows(v, r, keep)
            c_s[r0:r0 + ROW_BLOCK, lanes] = out
        after_block()

    p_s[0:rows_in, :] = jnp.where(valid, _dot(h_s[...], win_ref[:, 2 * da:]), 0.0)
    for r0 in range(0, ts, ROW_BLOCK):
        ln = _layernorm(c_s[r0:r0 + ROW_BLOCK, :], lg_ref[...], lb_ref[...])
        y_s[r0:r0 + ROW_BLOCK, 0:da] = (ln * jax.nn.sigmoid(ln)).astype(BF16)
        t = j * ts + r0 + lax.broadcasted_iota(jnp.int32, (ROW_BLOCK, 1), 0)
        for gi, w in enumerate(POOL_WINDOWS):
            half = w // 2
            doublings = half.bit_length() - 1
            lanes = slice(gi * gp, (gi + 1) * gp)
            xw = p_s[r0 + HALO - SUBLANES:r0 + HALO + ROW_BLOCK + SUBLANES * doublings, lanes]
            d = xw
            for i in range(doublings):
                d = d[:-SUBLANES] + _shift_rows(d, 1 << i, keep)
            s = _shift_rows(d, SUBLANES - half, keep) + d[SUBLANES:]
            cnt = (jnp.minimum(t + half - 1, seq - 1) - jnp.maximum(t - half, 0) + 1).astype(F32)
            pooled = s / cnt - xw[SUBLANES:SUBLANES + ROW_BLOCK]
            pl_s[r0:r0 + ROW_BLOCK, lanes] = pooled.astype(BF16)


def _even_back(xm_ref, pw_ref, ps_ref, wout_ref, y_s, pl_s):
    da = pl_s.shape[1]
    y_s[:, da:] = (_dot(pl_s[...], pw_ref[...]) * ps_ref[...]).astype(BF16)
    return xm_ref[...] + _dot(y_s[...], wout_ref[...])


def _layer0_kernel(xm_ref, xp_ref, xn_ref, mg_ref, win_ref, cw_ref, cb_ref, lg_ref, lb_ref,
                   pw_ref, ps_ref, wout_ref, fg_ref, wg_ref, wu_ref, wd_ref, o_ref,
                   h_s, u_s, p_s, c_s, y_s, pl_s, carry_s, hcarry_s, *, ts, seq, n_tiles):
    g = pl.program_id(0)

    @pl.when(g == 0)
    def _():
        carry_s[...] = jnp.zeros_like(carry_s)
        hcarry_s[...] = jnp.zeros_like(hcarry_s)
        p_s[ts + 2 * HALO:, :] = jnp.zeros_like(p_s[ts + 2 * HALO:, :])

    j = jnp.minimum(g, n_tiles - 1) % (seq // ts)
    partial = _ffn_partial_sums(carry_s[...], hcarry_s[...], wg_ref, wu_ref, wd_ref)
    sums = []
    per_block = iter(FFN_CHUNKS_AFTER_BLOCK)

    def ffn_chunk():
        for _ in range(next(per_block, 0)):
            sums.append(next(partial))

    _even_mixer_branches(j, xm_ref, xp_ref, xn_ref, mg_ref, win_ref, cw_ref, cb_ref, lg_ref, lb_ref,
                         h_s, u_s, p_s, c_s, y_s, pl_s, ts=ts, seq=seq, after_block=ffn_chunk)
    sums.extend(partial)
    o_ref[...] = sums[-1]
    x1 = _even_back(xm_ref, pw_ref, ps_ref, wout_ref, y_s, pl_s)
    carry_s[...] = x1
    hcarry_s[...] = _ffn_in(x1, fg_ref)


def _layer0_call(x2d, seq, mix_g, w_in, conv_w, conv_b, ln_g, ln_b, pool_bd, pool_scale, w_out,
                 ffn_g, wg, wu, wd, ts):
    t, d = x2d.shape
    da = conv_w.shape[1]
    nh = ts // HALO
    n_tiles = t // ts
    last_halo = t // HALO - 1
    glu = w_in[:, :2 * da].reshape(d, 2, da // LANE_BLOCK, LANE_BLOCK).transpose(0, 2, 1, 3)
    w_in = jnp.concatenate([glu.reshape(d, 2 * da), w_in[:, 2 * da:]], axis=1)

    def tile(g):
        return jnp.minimum(g, n_tiles - 1)

    kernel = functools.partial(_layer0_kernel, ts=ts, seq=seq, n_tiles=n_tiles)
    return pl.pallas_call(
        kernel,
        grid=(n_tiles + 1,),
        in_specs=[
            pl.BlockSpec((ts, d), lambda g: (tile(g), 0)),
            pl.BlockSpec((HALO, d), lambda g: (jnp.maximum(tile(g) * nh - 1, 0), 0)),
            pl.BlockSpec((HALO, d), lambda g: (jnp.minimum((tile(g) + 1) * nh, last_halo), 0)),
            _const_spec((1, d)), _const_spec(w_in.shape), _const_spec(conv_w.shape),
            _const_spec((1, da)), _const_spec((1, da)), _const_spec((1, da)),
            _const_spec(pool_bd.shape), _const_spec((1, da)), _const_spec(w_out.shape),
            _const_spec((1, d)), _const_spec(wg.shape, 0), _const_spec(wu.shape, 0), _const_spec(wd.shape, 0),
        ],
        out_specs=pl.BlockSpec((ts, d), lambda g: (jnp.maximum(g - 1, 0), 0)),
        out_shape=jax.ShapeDtypeStruct((t, d), F32),
        scratch_shapes=[
            pltpu.VMEM((ts + 2 * HALO, d), BF16),
            pltpu.VMEM((ts + 2 * HALO, da), F32),
            pltpu.VMEM((ts + 2 * HALO + SUBLANES, da), F32),
            pltpu.VMEM((ts, da), F32),
            pltpu.VMEM((ts, d), BF16),
            pltpu.VMEM((ts, da), BF16),
            pltpu.VMEM((ts, d), F32),
            pltpu.VMEM((ts, d), BF16),
        ],
        compiler_params=_params(("arbitrary",)),
        name="layer0",
    )(x2d, x2d, x2d, mix_g.reshape(1, d), w_in, conv_w, conv_b.reshape(1, da), ln_g.reshape(1, da),
      ln_b.reshape(1, da), pool_bd, pool_scale.reshape(1, da), w_out, ffn_g.reshape(1, d), wg, wu, wd)


def _odd_in_kernel(x_ref, g_ref, win_ref, cm_ref, vg_ref, vb_ref, sw_ref, sb_ref,
                   abo_ref, abe_ref, yd_ref, ab_s, *, ts):
    dc = vg_ref.shape[1]
    hd = dc // N_HEADS
    pair = cm_ref.shape[2]
    h = _rms(x_ref[...], g_ref[...]).astype(BF16)

    def gelu(zz):
        return 0.5 * zz * (1.0 + lax.erf(zz * math.sqrt(0.5)))

    v = gelu(_dot(h, win_ref[:, 2 * dc:]))
    c = _dot(h, win_ref[:, :dc]).astype(BF16)
    for q in range(2):
        for pi in range(dc // pair):
            part = _dot(c[:, pi * pair:(pi + 1) * pair], cm_ref[q, pi])
            for sub in range(pair // LANES):
                blk = pi * (pair // LANES) + sub
                lanes = slice(blk * LANES, (blk + 1) * LANES)
                ab_s[blk] = part[:, sub * LANES:(sub + 1) * LANES]
                abo_ref[q, :, lanes] = ab_s[blk, pl.ds(1, ts // 2, stride=2), :].astype(BF16)
                for cls in range(2):
                    abe_ref[q, cls, :, lanes] = ab_s[blk, pl.ds(2 * cls, ts // 4, stride=4), :].astype(BF16)
    u = gelu(_dot(h, win_ref[:, dc:2 * dc]))
    for hi in range(N_HEADS):
        lanes = slice(hi * hd, (hi + 1) * hd)
        vn = _layernorm(v[:, lanes], vg_ref[:, lanes], vb_ref[:, lanes]).astype(BF16)
        for n in range(ts // CHUNK):
            rows = slice(n * CHUNK, (n + 1) * CHUNK)
            sv = _dot(sw_ref[hi], vn[rows, :]) + sb_ref[:, lanes]
            yd_ref[rows, lanes] = (u[rows, lanes] * sv).astype(BF16)


def _odd_in_call(x, g, w_in, cmix, v_ln_g, v_ln_b, spatial_w, sb_full, ts):
    b, s, d = x.shape
    dc = v_ln_g.shape[0]
    kernel = functools.partial(_odd_in_kernel, ts=ts)
    return pl.pallas_call(
        kernel,
        grid=(b, s // ts),
        in_specs=[
            pl.BlockSpec((None, ts, d), lambda bi, j: (bi, j, 0)),
            _const_spec((1, d)), _const_spec(w_in.shape), _const_spec(cmix.shape),
            _const_spec((1, dc)), _const_spec((1, dc)),
            _const_spec(spatial_w.shape), _const_spec(sb_full.shape),
        ],
        out_specs=[
            pl.BlockSpec((2, ts // 2, dc), lambda bi, j: (0, j, bi)),
            pl.BlockSpec((2, 2, ts // 4, dc), lambda bi, j: (0, 0, j, bi)),
            pl.BlockSpec((None, ts, dc), lambda bi, j: (bi, j, 0)),
        ],
        out_shape=[
            jax.ShapeDtypeStruct((2, s // 2, b * dc), BF16),
            jax.ShapeDtypeStruct((2, 2, s // 4, b * dc), BF16),
            jax.ShapeDtypeStruct((b, s, dc), BF16),
        ],
        scratch_shapes=[pltpu.VMEM((dc // LANES, ts, LANES), F32)],
        compiler_params=_params(("parallel", "parallel")),
        name="odd_mixer_in",
    )(x, g.reshape(1, d), w_in, cmix, v_ln_g.reshape(1, dc), v_ln_b.reshape(1, dc), spatial_w, sb_full)


def _dft_kernel(ce_ref, co_lo_ref, co_hi_ref, abe_ref, abo_ref, o_ref, *, scale):
    def cos_minus_sin(m_ref, a, b):
        return _dot(m_ref[0], a) + _dot(m_ref[1], b)

    ee = cos_minus_sin(ce_ref.at[0], abe_ref[0, 0], abe_ref[1, 0])
    eo = cos_minus_sin(ce_ref.at[1], abe_ref[0, 1], abe_ref[1, 1])
    o_lo = cos_minus_sin(co_lo_ref, abo_ref[0], abo_ref[1])
    o_hi = cos_minus_sin(co_hi_ref, abo_ref[0], abo_ref[1])
    e_lo = ee + eo
    e_hi = ee - eo
    for quarter, val in enumerate((e_lo + o_lo, e_hi + o_hi, e_lo - o_lo, e_hi - o_hi)):
        o_ref[quarter] = (val * scale).astype(o_ref.dtype)


def _dft_call(cm_even, cm_odd, ab_even, ab_odd, scale, tm, tn):
    _, half, nc = ab_odd.shape
    quarter = half // 2
    kernel = functools.partial(_dft_kernel, scale=scale)
    resident = pl.Buffered(1) if tm == quarter else None
    out = pl.pallas_call(
        kernel,
        grid=(quarter // tm, nc // tn),
        in_specs=[
            pl.BlockSpec((2, 2, tm, quarter), lambda i, n: (0, 0, i, 0), pipeline_mode=resident),
            pl.BlockSpec((None, 2, tm, half), lambda i, n: (0, 0, i, 0), pipeline_mode=resident),
            pl.BlockSpec((None, 2, tm, half), lambda i, n: (0, 0, quarter // tm + i, 0), pipeline_mode=resident),
            pl.BlockSpec((2, 2, quarter, tn), lambda i, n: (0, 0, 0, n)),
            pl.BlockSpec((2, half, tn), lambda i, n: (0, 0, n)),
        ],
        out_specs=pl.BlockSpec((4, tm, tn), lambda i, n: (0, i, n)),
        out_shape=jax.ShapeDtypeStruct((4, quarter, nc), BF16),
        compiler_params=_params(("parallel", "parallel")),
        name="seq_dft",
    )(cm_even, cm_odd, cm_odd, ab_even, ab_odd)
    return out.reshape(4 * quarter, nc)


def _layer1_out_kernel(x_ref, yc_ref, yd_ref, wout_ref, fg_ref, wg_ref, wu_ref, wd_ref, ng_ref, o_ref):
    half = x_ref.shape[0] // 2
    x1s = []
    for r0 in (0, half):
        rows = slice(r0, r0 + half)
        y = jnp.concatenate([yc_ref[rows, :], yd_ref[rows, :]], axis=-1)
        x1s.append(x_ref[rows, :] + _dot(y, wout_ref[...]))
    for r0, x1 in zip((0, half), x1s):
        o_ref[r0:r0 + half, :] = _rms(_ffn(x1, _ffn_in(x1, fg_ref), wg_ref, wu_ref, wd_ref), ng_ref[...])


def _layer1_out_call(x, yc, yd, w_out, ffn_g, wg, wu, wd, final_g, ts):
    b, s, d = x.shape
    dc = yd.shape[2]
    return pl.pallas_call(
        _layer1_out_kernel,
        grid=(b, s // ts),
        in_specs=[
            pl.BlockSpec((None, ts, d), lambda bi, j: (bi, j, 0)),
            pl.BlockSpec((ts, dc), lambda bi, j: (j, bi)),
            pl.BlockSpec((None, ts, dc), lambda bi, j: (bi, j, 0)),
            _const_spec(w_out.shape),
            _const_spec((1, d)), _const_spec(wg.shape, 1), _const_spec(wu.shape, 1), _const_spec(wd.shape, 1),
            _const_spec((1, d)),
        ],
        out_specs=pl.BlockSpec((None, ts, d), lambda bi, j: (bi, j, 0)),
        out_shape=jax.ShapeDtypeStruct((b, s, d), F32),
        compiler_params=_params(("parallel", "parallel")),
        name="layer1_out",
    )(x, yc, yd, w_out, ffn_g.reshape(1, d), wg, wu, wd, final_g.reshape(1, d))


def _block_diag(w):
    g, n, _ = w.shape
    eye = jnp.eye(g, dtype=w.dtype)
    return (eye[:, None, :, None] * w[:, :, None, :]).reshape(g * n, g * n)


def _channel_dft_mats(n):
    idx = np.outer(np.arange(n), np.arange(n)) % n
    ang = 2.0 * np.pi * idx / n
    return np.stack([np.cos(ang), np.sin(ang)]).astype(np.float32)


def _fold_kernel(t_ref, w_ref, o_ref):
    n = t_ref.shape[1]
    o_ref[...] = jnp.zeros_like(o_ref)
    for q in range(2):
        for pi in range(o_ref.shape[1]):
            for sub in range(o_ref.shape[2] // n):
                blk = slice(sub * n, (sub + 1) * n)
                head = pi * (o_ref.shape[2] // n) + sub
                o_ref[q, pi, blk, blk] = jnp.dot(t_ref[q], w_ref[head], preferred_element_type=F32,
                                                 precision=lax.Precision.HIGHEST).astype(o_ref.dtype)


def _fold_channel_mix(fourier_w):
    heads, n, _ = fourier_w.shape
    per = MXU_DIM // n
    return pl.pallas_call(
        _fold_kernel,
        out_shape=jax.ShapeDtypeStruct((2, heads // per, per * n, per * n), BF16),
        name="fold_channel_mix",
    )(jnp.asarray(_channel_dft_mats(n)), fourier_w)


def _twiddle_kernel(ta_ref, tb_ref, o_ref):
    for blk in range(o_ref.shape[3] // LANES):
        ca = ta_ref[0, :, blk:blk + 1]
        sa = ta_ref[1, :, blk:blk + 1]
        cols = slice(blk * LANES, (blk + 1) * LANES)
        for pi in range(o_ref.shape[0]):
            cb = tb_ref[pi, 0]
            sb = tb_ref[pi, 1]
            o_ref[pi, 0, :, cols] = (ca * cb - sa * sb).astype(o_ref.dtype)
            o_ref[pi, 1, :, cols] = (-(sa * cb) - ca * sb).astype(o_ref.dtype)


def _seq_dft_mats(s, rows, parities):
    n_par = len(parities)
    half = s // 2
    nblk = half // LANES
    unit = 2.0 * math.pi / s

    def cos_sin(idx):
        ang = (idx % s).astype(F32) * unit
        return jnp.stack([jnp.cos(ang), jnp.sin(ang)])

    j = lax.broadcasted_iota(jnp.int32, (half, nblk), 0)
    blk = lax.broadcasted_iota(jnp.int32, (half, nblk), 1)
    ta = cos_sin(j * (2 * LANES * blk))
    j = lax.broadcasted_iota(jnp.int32, (half, LANES), 0)
    ml = lax.broadcasted_iota(jnp.int32, (half, LANES), 1)
    tb = jnp.stack([cos_sin(j * (2 * ml + parity)) for parity in parities])
    return pl.pallas_call(
        _twiddle_kernel,
        grid=(half // rows,),
        in_specs=[pl.BlockSpec((2, rows, nblk), lambda i: (0, i, 0)),
                  pl.BlockSpec((n_par, 2, rows, LANES), lambda i: (0, 0, i, 0))],
        out_specs=pl.BlockSpec((n_par, 2, rows, half), lambda i: (0, 0, i, 0)),
        out_shape=jax.ShapeDtypeStruct((n_par, 2, half, half), BF16),
        compiler_params=_params(("parallel",)),
        name=f"seq_dft_twiddles_{s}",
    )(ta, tb)


def kernel(x, mix_norm_g, ffn_norm_g, ev_w_in, ev_conv_w, ev_conv_b, ev_ln_g, ev_ln_b, ev_pool_w,
           ev_pool_scale, ev_w_out, od_w_in, od_fourier_w, od_v_ln_g, od_v_ln_b, od_spatial_w,
           od_spatial_b, od_w_out, ffn_w_gate, ffn_w_up, ffn_w_down, final_norm_g):
    b, s, d = x.shape
    assert mix_norm_g.shape[0] == 2, "one conv/pool layer followed by one fourier/gating layer"
    hc = od_fourier_w.shape[2]
    ts = 512
    ts1 = 1024

    cm_odd = _seq_dft_mats(s, 256, (1,))
    cm_even = _seq_dft_mats(s // 2, 256, (0, 1))
    dft_scale = 1.0 / math.sqrt(s * hc)
    wg, wu, wd = (w.astype(BF16) for w in (ffn_w_gate, ffn_w_up, ffn_w_down))

    x = _layer0_call(x.reshape(b * s, d), s, mix_norm_g[0], ev_w_in[0].astype(BF16), ev_conv_w[0],
                     ev_conv_b[0], ev_ln_g[0], ev_ln_b[0], _block_diag(ev_pool_w[0]).astype(BF16),
                     ev_pool_scale[0].reshape(-1), ev_w_out[0].astype(BF16),
                     ffn_norm_g[0], wg, wu, wd, ts).reshape(b, s, d)

    sb_full = jnp.broadcast_to(od_spatial_b[0].T[:, :, None], (CHUNK, N_HEADS, hc))
    ab_odd, ab_even, yd = _odd_in_call(
        x, mix_norm_g[1], od_w_in[0].astype(BF16), _fold_channel_mix(od_fourier_w[0]),
        od_v_ln_g[0].reshape(-1), od_v_ln_b[0].reshape(-1), od_spatial_w[0].astype(BF16),
        sb_full.reshape(CHUNK, N_HEADS * hc), ts1)
    yc = _dft_call(cm_even, cm_odd, ab_even, ab_odd, dft_scale, s // 4, 256)
    return _layer1_out_call(x, yc, yd, od_w_out[0].astype(BF16), ffn_norm_g[1], wg, wu, wd,
                            final_norm_g, ts1)
```

```python
import functools
import math

import numpy as np
import jax
import jax.numpy as jnp
from jax import lax
from jax.experimental import pallas as pl
from jax.experimental.pallas import tpu as pltpu

EPS = 1e-6
CONV_WIDTH = 31
CONV_PAD = CONV_WIDTH // 2
POOL_WINDOWS = (2, 4, 8, 16)
N_HEADS = 4
CHUNK = 128
HALO = 16
ROW_BLOCK = 64
LANE_BLOCK = 128
FFN_CHUNKS_AFTER_BLOCK = (0, 0, 2, 0)
SUBLANES = 8
LANES = 128
MXU_DIM = 256
VMEM_LIMIT = 56 * 1024 * 1024

F32 = jnp.float32
BF16 = jnp.bfloat16


def _rms(xv, g):
    ms = jnp.mean(xv * xv, axis=-1, keepdims=True)
    return xv * lax.rsqrt(ms + EPS) * g


def _layernorm(xv, g, b):
    mu = jnp.mean(xv, axis=-1, keepdims=True)
    d = xv - mu
    var = jnp.mean(d * d, axis=-1, keepdims=True)
    return d * lax.rsqrt(var + EPS) * g + b


def _dot(a, b):
    return jnp.dot(a, b, preferred_element_type=F32)


def _const_spec(shape, layer=None):
    if layer is None:
        return pl.BlockSpec(shape, lambda *_: (0,) * len(shape), pipeline_mode=pl.Buffered(1))
    return pl.BlockSpec((None,) + tuple(shape[1:]), lambda *_: (layer,) + (0,) * (len(shape) - 1),
                        pipeline_mode=pl.Buffered(1))


def _params(semantics):
    return pltpu.CompilerParams(dimension_semantics=semantics, vmem_limit_bytes=VMEM_LIMIT)


def _row_masks():
    sub = lax.broadcasted_iota(jnp.int32, (SUBLANES, LANES), 0)
    return {k: sub < SUBLANES - k for k in range(1, SUBLANES)}


def _shift_rows(a, k, keep):
    if k == 0:
        return a[:-SUBLANES]
    n, w = a.shape
    tiles = pltpu.roll(a.reshape(n // SUBLANES, SUBLANES, w), SUBLANES - k, axis=1)
    return jnp.concatenate([jnp.where(keep[k], tiles[t], tiles[t + 1]) for t in range(n // SUBLANES - 1)],
                           axis=0)


def _ffn_chunks(d_ff, rows):
    step = max(MXU_DIM, (1 << 19) // rows // MXU_DIM * MXU_DIM)
    return tuple((c, min(c + step, d_ff)) for c in range(0, d_ff, step))


def _ffn_in(x, g_ref):
    return _rms(x, g_ref[...]).astype(BF16)


def _ffn_partial_sums(x, h, wg_ref, wu_ref, wd_ref):
    acc = x
    for c0, c1 in _ffn_chunks(wg_ref.shape[1], x.shape[0]):
        gate = _dot(h, wg_ref[:, c0:c1])
        up = _dot(h, wu_ref[:, c0:c1])
        a = (gate * jax.nn.sigmoid(gate) * up).astype(BF16)
        acc = acc + _dot(a, wd_ref[c0:c1, :])
        yield acc


def _ffn(x, h, wg_ref, wu_ref, wd_ref):
    *_, out = _ffn_partial_sums(x, h, wg_ref, wu_ref, wd_ref)
    return out


def _even_mixer_branches(j, xm_ref, xp_ref, xn_ref, g_ref, win_ref, cw_ref, cb_ref, lg_ref, lb_ref,
                         h_s, u_s, p_s, c_s, y_s, pl_s, *, ts, seq, after_block=lambda: None):
    da = u_s.shape[1]
    gp = da // N_HEADS
    rows_in = ts + 2 * HALO
    keep = _row_masks()
    g = g_ref[...]
    h_s[0:HALO, :] = _rms(xp_ref[...], g).astype(BF16)
    h_s[HALO:HALO + ts, :] = _rms(xm_ref[...], g).astype(BF16)
    h_s[HALO + ts:, :] = _rms(xn_ref[...], g).astype(BF16)
    pos = j * ts - HALO + lax.broadcasted_iota(jnp.int32, (rows_in, 1), 0)
    valid = (pos >= 0) & (pos < seq)

    for l0 in range(0, da, LANE_BLOCK):
        lanes = slice(l0, l0 + LANE_BLOCK)
        z = _dot(h_s[...], win_ref[:, 2 * l0:2 * (l0 + LANE_BLOCK)])
        u_s[:, lanes] = jnp.where(valid, z[:, :LANE_BLOCK] * jax.nn.sigmoid(z[:, LANE_BLOCK:]), 0.0)
        for r0 in range(0, ts, ROW_BLOCK):
            out = jnp.broadcast_to(cb_ref[:, lanes], (ROW_BLOCK, LANE_BLOCK))
            for r in range(SUBLANES):
                v = None
                for q in range(pl.cdiv(CONV_WIDTH + HALO - CONV_PAD, SUBLANES)):
                    k = SUBLANES * q + r - (HALO - CONV_PAD)
                    if 0 <= k < CONV_WIDTH:
                        term = (u_s[r0 + SUBLANES * q:r0 + SUBLANES * q + ROW_BLOCK + SUBLANES, lanes]
                                * cw_ref[k:k + 1, lanes])
                        v = term if v is None else v + term
                out = out + _shift_rows(v, r, keep)
            c_s[r0:r0 + ROW_BLOCK, lanes] = out
        after_block()

    p_s[0:rows_in, :] = jnp.where(valid, _dot(h_s[...], win_ref[:, 2 * da:]), 0.0)
    for r0 in range(0, ts, ROW_BLOCK):
        ln = _layernorm(c_s[r0:r0 + ROW_BLOCK, :], lg_ref[...], lb_ref[...])
        y_s[r0:r0 + ROW_BLOCK, 0:da] = (ln * jax.nn.sigmoid(ln)).astype(BF16)
        t = j * ts + r0 + lax.broadcasted_iota(jnp.int32, (ROW_BLOCK, 1), 0)
        for gi, w in enumerate(POOL_WINDOWS):
            half = w // 2
            doublings = half.bit_length() - 1
            lanes = slice(gi * gp, (gi + 1) * gp)
            xw = p_s[r0 + HALO - SUBLANES:r0 + HALO + ROW_BLOCK + SUBLANES * doublings, lanes]
            d = xw
            for i in range(doublings):
                d = d[:-SUBLANES] + _shift_rows(d, 1 << i, keep)
            s = _shift_rows(d, SUBLANES - half, keep) + d[SUBLANES:]
            cnt = (jnp.minimum(t + half - 1, seq - 1) - jnp.maximum(t - half, 0) + 1).astype(F32)
            pooled = s / cnt - xw[SUBLANES:SUBLANES + ROW_BLOCK]
            pl_s[r0:r0 + ROW_BLOCK, lanes] = pooled.astype(BF16)


def _even_back(xm_ref, pw_ref, ps_ref, wout_ref, y_s, pl_s):
    da = pl_s.shape[1]
    y_s[:, da:] = (_dot(pl_s[...], pw_ref[...]) * ps_ref[...]).astype(BF16)
    return xm_ref[...] + _dot(y_s[...], wout_ref[...])


def _layer0_kernel(xm_ref, xp_ref, xn_ref, mg_ref, win_ref, cw_ref, cb_ref, lg_ref, lb_ref,
                   pw_ref, ps_ref, wout_ref, fg_ref, wg_ref, wu_ref, wd_ref, o_ref,
                   h_s, u_s, p_s, c_s, y_s, pl_s, carry_s, hcarry_s, *, ts, seq, n_tiles):
    g = pl.program_id(0)

    @pl.when(g == 0)
    def _():
        carry_s[...] = jnp.zeros_like(carry_s)
        hcarry_s[...] = jnp.zeros_like(hcarry_s)
        p_s[ts + 2 * HALO:, :] = jnp.zeros_like(p_s[ts + 2 * HALO:, :])

    j = jnp.minimum(g, n_tiles - 1) % (seq // ts)
    partial = _ffn_partial_sums(carry_s[...], hcarry_s[...], wg_ref, wu_ref, wd_ref)
    sums = []
    per_block = iter(FFN_CHUNKS_AFTER_BLOCK)

    def ffn_chunk():
        for _ in range(next(per_block, 0)):
            sums.append(next(partial))

    _even_mixer_branches(j, xm_ref, xp_ref, xn_ref, mg_ref, win_ref, cw_ref, cb_ref, lg_ref, lb_ref,
                         h_s, u_s, p_s, c_s, y_s, pl_s, ts=ts, seq=seq, after_block=ffn_chunk)
    sums.extend(partial)
    o_ref[...] = sums[-1]
    x1 = _even_back(xm_ref, pw_ref, ps_ref, wout_ref, y_s, pl_s)
    carry_s[...] = x1
    hcarry_s[...] = _ffn_in(x1, fg_ref)


def _layer0_call(x2d, seq, mix_g, w_in, conv_w, conv_b, ln_g, ln_b, pool_bd, pool_scale, w_out,
                 ffn_g, wg, wu, wd, ts):
    t, d = x2d.shape
    da = conv_w.shape[1]
    nh = ts // HALO
    n_tiles = t // ts
    last_halo = t // HALO - 1
    glu = w_in[:, :2 * da].reshape(d, 2, da // LANE_BLOCK, LANE_BLOCK).transpose(0, 2, 1, 3)
    w_in = jnp.concatenate([glu.reshape(d, 2 * da), w_in[:, 2 * da:]], axis=1)

    def tile(g):
        return jnp.minimum(g, n_tiles - 1)

    kernel = functools.partial(_layer0_kernel, ts=ts, seq=seq, n_tiles=n_tiles)
    return pl.pallas_call(
        kernel,
        grid=(n_tiles + 1,),
        in_specs=[
            pl.BlockSpec((ts, d), lambda g: (tile(g), 0)),
            pl.BlockSpec((HALO, d), lambda g: (jnp.maximum(tile(g) * nh - 1, 0), 0)),
            pl.BlockSpec((HALO, d), lambda g: (jnp.minimum((tile(g) + 1) * nh, last_halo), 0)),
            _const_spec((1, d)), _const_spec(w_in.shape), _const_spec(conv_w.shape),
            _const_spec((1, da)), _const_spec((1, da)), _const_spec((1, da)),
            _const_spec(pool_bd.shape), _const_spec((1, da)), _const_spec(w_out.shape),
            _const_spec((1, d)), _const_spec(wg.shape, 0), _const_spec(wu.shape, 0), _const_spec(wd.shape, 0),
        ],
        out_specs=pl.BlockSpec((ts, d), lambda g: (jnp.maximum(g - 1, 0), 0)),
        out_shape=jax.ShapeDtypeStruct((t, d), F32),
        scratch_shapes=[
            pltpu.VMEM((ts + 2 * HALO, d), BF16),
            pltpu.VMEM((ts + 2 * HALO, da), F32),
            pltpu.VMEM((ts + 2 * HALO + SUBLANES, da), F32),
            pltpu.VMEM((ts, da), F32),
            pltpu.VMEM((ts, d), BF16),
            pltpu.VMEM((ts, da), BF16),
            pltpu.VMEM((ts, d), F32),
            pltpu.VMEM((ts, d), BF16),
        ],
        compiler_params=_params(("arbitrary",)),
        name="layer0",
    )(x2d, x2d, x2d, mix_g.reshape(1, d), w_in, conv_w, conv_b.reshape(1, da), ln_g.reshape(1, da),
      ln_b.reshape(1, da), pool_bd, pool_scale.reshape(1, da), w_out, ffn_g.reshape(1, d), wg, wu, wd)


def _odd_in_kernel(x_ref, g_ref, win_ref, cm_ref, vg_ref, vb_ref, sw_ref, sb_ref,
                   abo_ref, abe_ref, yd_ref, ab_s, *, ts):
    dc = vg_ref.shape[1]
    hd = dc // N_HEADS
    pair = cm_ref.shape[2]
    h = _rms(x_ref[...], g_ref[...]).astype(BF16)

    def gelu(zz):
        return 0.5 * zz * (1.0 + lax.erf(zz * math.sqrt(0.5)))

    v = gelu(_dot(h, win_ref[:, 2 * dc:]))
    c = _dot(h, win_ref[:, :dc]).astype(BF16)
    for q in range(2):
        for pi in range(dc // pair):
            part = _dot(c[:, pi * pair:(pi + 1) * pair], cm_ref[q, pi])
            for sub in range(pair // LANES):
                blk = pi * (pair // LANES) + sub
                lanes = slice(blk * LANES, (blk + 1) * LANES)
                ab_s[blk] = part[:, sub * LANES:(sub + 1) * LANES]
                abo_ref[q, :, lanes] = ab_s[blk, pl.ds(1, ts // 2, stride=2), :].astype(BF16)
                for cls in range(2):
                    abe_ref[q, cls, :, lanes] = ab_s[blk, pl.ds(2 * cls, ts // 4, stride=4), :].astype(BF16)
    u = gelu(_dot(h, win_ref[:, dc:2 * dc]))
    for hi in range(N_HEADS):
        lanes = slice(hi * hd, (hi + 1) * hd)
        vn = _layernorm(v[:, lanes], vg_ref[:, lanes], vb_ref[:, lanes]).astype(BF16)
        chunks = jnp.concatenate([vn[n * CHUNK:(n + 1) * CHUNK, :] for n in range(ts // CHUNK)], axis=1)
        sv_all = _dot(sw_ref[hi], chunks)
        for n in range(ts // CHUNK):
            rows = slice(n * CHUNK, (n + 1) * CHUNK)
            sv = sv_all[:, n * hd:(n + 1) * hd] + sb_ref[:, lanes]
            yd_ref[rows, lanes] = (u[rows, lanes] * sv).astype(BF16)


def _odd_in_call(x, g, w_in, cmix, v_ln_g, v_ln_b, spatial_w, sb_full, ts):
    b, s, d = x.shape
    dc = v_ln_g.shape[0]
    kernel = functools.partial(_odd_in_kernel, ts=ts)
    return pl.pallas_call(
        kernel,
        grid=(b, s // ts),
        in_specs=[
            pl.BlockSpec((None, ts, d), lambda bi, j: (bi, j, 0)),
            _const_spec((1, d)), _const_spec(w_in.shape), _const_spec(cmix.shape),
            _const_spec((1, dc)), _const_spec((1, dc)),
            _const_spec(spatial_w.shape), _const_spec(sb_full.shape),
        ],
        out_specs=[
            pl.BlockSpec((2, ts // 2, dc), lambda bi, j: (0, j, bi)),
            pl.BlockSpec((2, 2, ts // 4, dc), lambda bi, j: (0, 0, j, bi)),
            pl.BlockSpec((None, ts, dc), lambda bi, j: (bi, j, 0)),
        ],
        out_shape=[
            jax.ShapeDtypeStruct((2, s // 2, b * dc), BF16),
            jax.ShapeDtypeStruct((2, 2, s // 4, b * dc), BF16),
            jax.ShapeDtypeStruct((b, s, dc), BF16),
        ],
        scratch_shapes=[pltpu.VMEM((dc // LANES, ts, LANES), F32)],
        compiler_params=_params(("parallel", "parallel")),
        name="odd_mixer_in",
    )(x, g.reshape(1, d), w_in, cmix, v_ln_g.reshape(1, dc), v_ln_b.reshape(1, dc), spatial_w, sb_full)


def _dft_kernel(ce_ref, co_lo_ref, co_hi_ref, abe_ref, abo_ref, o_ref, *, scale):
    def cos_minus_sin(m_ref, a, b):
        return _dot(m_ref[0], a) + _dot(m_ref[1], b)

    ee = cos_minus_sin(ce_ref.at[0], abe_ref[0, 0], abe_ref[1, 0])
    eo = cos_minus_sin(ce_ref.at[1], abe_ref[0, 1], abe_ref[1, 1])
    o_lo = cos_minus_sin(co_lo_ref, abo_ref[0], abo_ref[1])
    o_hi = cos_minus_sin(co_hi_ref, abo_ref[0], abo_ref[1])
    e_lo = ee + eo
    e_hi = ee - eo
    for quarter, val in enumerate((e_lo + o_lo, e_hi + o_hi, e_lo - o_lo, e_hi - o_hi)):
        o_ref[quarter] = (val * scale).astype(o_ref.dtype)


def _dft_call(cm_even, cm_odd, ab_even, ab_odd, scale, tm, tn):
    _, half, nc = ab_odd.shape
    quarter = half // 2
    kernel = functools.partial(_dft_kernel, scale=scale)
    out = pl.pallas_call(
        kernel,
        grid=(quarter // tm, nc // tn),
        in_specs=[
            pl.BlockSpec((2, 2, tm, quarter), lambda i, n: (0, 0, i, 0)),
            pl.BlockSpec((None, 2, tm, half), lambda i, n: (0, 0, i, 0)),
            pl.BlockSpec((None, 2, tm, half), lambda i, n: (0, 0, quarter // tm + i, 0)),
            pl.BlockSpec((2, 2, quarter, tn), lambda i, n: (0, 0, 0, n)),
            pl.BlockSpec((2, half, tn), lambda i, n: (0, 0, n)),
        ],
        out_specs=pl.BlockSpec((4, tm, tn), lambda i, n: (0, i, n)),
        out_shape=jax.ShapeDtypeStruct((4, quarter, nc), BF16),
        compiler_params=_params(("parallel", "parallel")),
        name="seq_dft",
    )(cm_even, cm_odd, cm_odd, ab_even, ab_odd)
    return out.reshape(4 * quarter, nc)


def _layer1_out_kernel(x_ref, yc_ref, yd_ref, wout_ref, fg_ref, wg_ref, wu_ref, wd_ref, ng_ref, o_ref):
    half = x_ref.shape[0] // 2
    x1s = []
    for r0 in (0, half):
        rows = slice(r0, r0 + half)
        y = jnp.concatenate([yc_ref[rows, :], yd_ref[rows, :]], axis=-1)
        x1s.append(x_ref[rows, :] + _dot(y, wout_ref[...]))
    for r0, x1 in zip((0, half), x1s):
        o_ref[r0:r0 + half, :] = _rms(_ffn(x1, _ffn_in(x1, fg_ref), wg_ref, wu_ref, wd_ref), ng_ref[...])


def _layer1_out_call(x, yc, yd, w_out, ffn_g, wg, wu, wd, final_g, ts):
    b, s, d = x.shape
    dc = yd.shape[2]
    return pl.pallas_call(
        _layer1_out_kernel,
        grid=(b, s // ts),
        in_specs=[
            pl.BlockSpec((None, ts, d), lambda bi, j: (bi, j, 0)),
            pl.BlockSpec((ts, dc), lambda bi, j: (j, bi)),
            pl.BlockSpec((None, ts, dc), lambda bi, j: (bi, j, 0)),
            _const_spec(w_out.shape),
            _const_spec((1, d)), _const_spec(wg.shape, 1), _const_spec(wu.shape, 1), _const_spec(wd.shape, 1),
            _const_spec((1, d)),
        ],
        out_specs=pl.BlockSpec((None, ts, d), lambda bi, j: (bi, j, 0)),
        out_shape=jax.ShapeDtypeStruct((b, s, d), F32),
        compiler_params=_params(("parallel", "parallel")),
        name="layer1_out",
    )(x, yc, yd, w_out, ffn_g.reshape(1, d), wg, wu, wd, final_g.reshape(1, d))


def _block_diag(w):
    g, n, _ = w.shape
    eye = jnp.eye(g, dtype=w.dtype)
    return (eye[:, None, :, None] * w[:, :, None, :]).reshape(g * n, g * n)


def _channel_dft_mats(n):
    idx = np.outer(np.arange(n), np.arange(n)) % n
    ang = 2.0 * np.pi * idx / n
    return np.stack([np.cos(ang), np.sin(ang)]).astype(np.float32)


def _fold_kernel(t_ref, w_ref, o_ref):
    n = t_ref.shape[1]
    o_ref[...] = jnp.zeros_like(o_ref)
    for q in range(2):
        for pi in range(o_ref.shape[1]):
            for sub in range(o_ref.shape[2] // n):
                blk = slice(sub * n, (sub + 1) * n)
                head = pi * (o_ref.shape[2] // n) + sub
                o_ref[q, pi, blk, blk] = jnp.dot(t_ref[q], w_ref[head], preferred_element_type=F32,
                                                 precision=lax.Precision.HIGHEST).astype(o_ref.dtype)


def _fold_channel_mix(fourier_w):
    heads, n, _ = fourier_w.shape
    per = MXU_DIM // n
    return pl.pallas_call(
        _fold_kernel,
        out_shape=jax.ShapeDtypeStruct((2, heads // per, per * n, per * n), BF16),
        name="fold_channel_mix",
    )(jnp.asarray(_channel_dft_mats(n)), fourier_w)


def _twiddle_kernel(ta_ref, tb_ref, o_ref):
    for blk in range(o_ref.shape[3] // LANES):
        ca = ta_ref[0, :, blk:blk + 1]
        sa = ta_ref[1, :, blk:blk + 1]
        cols = slice(blk * LANES, (blk + 1) * LANES)
        for pi in range(o_ref.shape[0]):
            cb = tb_ref[pi, 0]
            sb = tb_ref[pi, 1]
            o_ref[pi, 0, :, cols] = (ca * cb - sa * sb).astype(o_ref.dtype)
            o_ref[pi, 1, :, cols] = (-(sa * cb) - ca * sb).astype(o_ref.dtype)


def _seq_dft_mats(s, rows, parities):
    n_par = len(parities)
    half = s // 2
    nblk = half // LANES
    unit = 2.0 * math.pi / s

    def cos_sin(idx):
        ang = (idx % s).astype(F32) * unit
        return jnp.stack([jnp.cos(ang), jnp.sin(ang)])

    j = lax.broadcasted_iota(jnp.int32, (half, nblk), 0)
    blk = lax.broadcasted_iota(jnp.int32, (half, nblk), 1)
    ta = cos_sin(j * (2 * LANES * blk))
    j = lax.broadcasted_iota(jnp.int32, (half, LANES), 0)
    ml = lax.broadcasted_iota(jnp.int32, (half, LANES), 1)
    tb = jnp.stack([cos_sin(j * (2 * ml + parity)) for parity in parities])
    return pl.pallas_call(
        _twiddle_kernel,
        grid=(half // rows,),
        in_specs=[pl.BlockSpec((2, rows, nblk), lambda i: (0, i, 0)),
                  pl.BlockSpec((n_par, 2, rows, LANES), lambda i: (0, 0, i, 0))],
        out_specs=pl.BlockSpec((n_par, 2, rows, half), lambda i: (0, 0, i, 0)),
        out_shape=jax.ShapeDtypeStruct((n_par, 2, half, half), BF16),
        compiler_params=_params(("parallel",)),
        name=f"seq_dft_twiddles_{s}",
    )(ta, tb)


def kernel(x, mix_norm_g, ffn_norm_g, ev_w_in, ev_conv_w, ev_conv_b, ev_ln_g, ev_ln_b, ev_pool_w,
           ev_pool_scale, ev_w_out, od_w_in, od_fourier_w, od_v_ln_g, od_v_ln_b, od_spatial_w,
           od_spatial_b, od_w_out, ffn_w_gate, ffn_w_up, ffn_w_down, final_norm_g):
    b, s, d = x.shape
    assert mix_norm_g.shape[0] == 2, "one conv/pool layer followed by one fourier/gating layer"
    hc = od_fourier_w.shape[2]
    ts = 512
    ts1 = 1024

    cm_odd = _seq_dft_mats(s, 256, (1,))
    cm_even = _seq_dft_mats(s // 2, 256, (0, 1))
    dft_scale = 1.0 / math.sqrt(s * hc)
    wg, wu, wd = (w.astype(BF16) for w in (ffn_w_gate, ffn_w_up, ffn_w_down))

    x = _layer0_call(x.reshape(b * s, d), s, mix_norm_g[0], ev_w_in[0].astype(BF16), ev_conv_w[0],
                     ev_conv_b[0], ev_ln_g[0], ev_ln_b[0], _block_diag(ev_pool_w[0]).astype(BF16),
                     ev_pool_scale[0].reshape(-1), ev_w_out[0].astype(BF16),
                     ffn_norm_g[0], wg, wu, wd, ts).reshape(b, s, d)

    sb_full = jnp.broadcast_to(od_spatial_b[0].T[:, :, None], (CHUNK, N_HEADS, hc))
    ab_odd, ab_even, yd = _odd_in_call(
        x, mix_norm_g[1], od_w_in[0].astype(BF16), _fold_channel_mix(od_fourier_w[0]),
        od_v_ln_g[0].reshape(-1), od_v_ln_b[0].reshape(-1), od_spatial_w[0].astype(BF16),
        sb_full.reshape(CHUNK, N_HEADS * hc), ts1)
    yc = _dft_call(cm_even, cm_odd, ab_even, ab_odd, dft_scale, 512, 512)
    return _layer1_out_call(x, yc, yd, od_w_out[0].astype(BF16), ffn_norm_g[1], wg, wu, wd,
                            final_norm_g, ts1)
```
